```python
import jax
import jax.numpy as jnp
from jax import lax
import numpy as np

D_MODEL = 1024
BATCH = 2
SEQ = 8192
DEPTH = 2
DEC_BATCH = 32
DEC_SEQ = 1
PAST_LEN = 16384
PAGE_SIZE = 128

ATTN_GROUPS = ((128, 1), (512, 4), (2048, 16))
N_GROUPS = 3
HEADS_PER_GROUP = 4
HD_A = 128
ATTN_BLK = 128
N_BUCKETS = 32
REL_MAX_DIST = 2048
HR = 4
DK_R = 128
DV_R = 256
RET_CHUNK = 128
N_KEYS = 128
N_EXPERTS = N_KEYS * N_KEYS
PEER_HEADS = 8
PEER_TOPK = 16
PEER_DQ = 256
PEER_BLK = 128
EPS = 1e-6
NEG = -1e30

QA_W = N_GROUPS * HEADS_PER_GROUP * HD_A
RQ_W = HR * DK_R
RV_W = HR * DV_R
ATTN_OUT_W = HEADS_PER_GROUP * HD_A
SPLITS = (QA_W, QA_W, QA_W, RQ_W, RQ_W, RV_W, RV_W, 2 * D_MODEL)
IN_COLS = sum(SPLITS)

kernel_name = 'dilated_attn_retention_peer_hybrid_step'


def _rms(x, w=None):
    xf = x.astype(jnp.float32)
    y = xf * lax.rsqrt(jnp.mean(xf * xf, axis=-1, keepdims=True) + EPS)
    return y if w is None else y * w.astype(jnp.float32)


def _t5_bucket(dist):
    dist = np.asarray(dist, np.int64)
    max_exact = N_BUCKETS // 2
    ratio = np.log(np.maximum(dist, 1) / max_exact) / np.log(REL_MAX_DIST / max_exact)
    large = np.minimum(max_exact + (ratio * (N_BUCKETS - max_exact)).astype(np.int64), N_BUCKETS - 1)
    return np.where(dist < max_exact, dist, large).astype(np.int32)


def _to_sub(x, d):
    b, s = x.shape[:2]
    ls = s // d
    lp = -(-ls // ATTN_BLK) * ATTN_BLK
    x = x.reshape((b, ls, d) + x.shape[2:])
    x = jnp.moveaxis(x, 2, 1).reshape((b * d, ls) + x.shape[3:])
    x = jnp.pad(x, [(0, 0), (0, lp - ls)] + [(0, 0)] * (x.ndim - 2))
    return x.reshape((b * d, lp // ATTN_BLK, ATTN_BLK) + x.shape[2:])


def _from_sub(y, b, s, d):
    ls = s // d
    y = y.reshape((b * d, -1) + y.shape[3:])[:, :ls]
    y = y.reshape((b, d, ls) + y.shape[2:])
    return jnp.moveaxis(y, 1, 2).reshape((b, s) + y.shape[3:])


def _dilated_prompt(q, k, v, bias_tab, window, d):
    b, s = q.shape[:2]
    steps = window // d
    qb, kb, vb = _to_sub(q, d), _to_sub(k, d), _to_sub(v, d)
    nb = qb.shape[1]

    def with_prev(t):
        prev = jnp.pad(t, [(0, 0), (1, 0)] + [(0, 0)] * (t.ndim - 2))[:, :-1]
        return jnp.concatenate([prev, t], axis=2)

    kk, vv = with_prev(kb), with_prev(vb)
    qi = np.arange(ATTN_BLK)[:, None]
    ki = np.arange(2 * ATTN_BLK)[None, :]
    m = ATTN_BLK + qi - ki
    band = (m >= 0) & (m <= steps)
    bias = jnp.transpose(bias_tab[_t5_bucket(np.clip(m, 0, steps) * d)], (2, 0, 1))
    mask = jnp.asarray(band)[None] & ((jnp.arange(nb)[:, None, None] > 0) | jnp.asarray(ki >= ATTN_BLK)[None])
    sc = jnp.einsum('nbqhd,nbkhd->nbhqk', qb, kk) + bias
    sc = jnp.where(mask[None, :, None], sc, NEG)
    lse = jax.nn.logsumexp(sc, axis=-1)
    p = jnp.exp(sc - lse[..., None])
    o = jnp.einsum('nbhqk,nbkhd->nbqhd', p, vv)
    lse = jnp.moveaxis(lse, 2, 3)
    return _from_sub(o, b, s, d), _from_sub(lse, b, s, d)


def _dilated_sample(q, k, v, kv_buf, bias_tab, window, d):
    t = q.shape[1]
    lg = kv_buf.shape[1]
    steps = window // d
    j = np.arange(steps + 1)
    idx = lg + np.arange(t)[:, None] - j[None, :] * d
    valid = jnp.asarray(idx >= 0)
    idx = np.maximum(idx, 0)
    keys = jnp.concatenate([kv_buf[:, :, 0].astype(jnp.float32), k], axis=1)
    vals = jnp.concatenate([kv_buf[:, :, 1].astype(jnp.float32), v], axis=1)
    kg, vg = keys[:, idx], vals[:, idx]
    bias = bias_tab[_t5_bucket(j * d)].T
    sc = jnp.einsum('bthd,btjhd->bthj', q, kg) + bias
    sc = jnp.where(valid[None, :, None, :], sc, NEG)
    lse = jax.nn.logsumexp(sc, axis=-1)
    p = jnp.exp(sc - lse[..., None])
    return jnp.einsum('bthj,btjhd->bthd', p, vg), lse


def _combine(outs, lses):
    w = jax.nn.softmax(jnp.stack(lses, 0), axis=0)
    return jnp.einsum('gbthd,gbth->bthd', jnp.stack(outs, 0), w)


def _attn_prompt(q, k, v, bias_g, out_dtype):
    outs, lses, rows = [], [], []
    for g, (window, dil) in enumerate(ATTN_GROUPS):
        o, l = _dilated_prompt(q[:, :, g], k[:, :, g], v[:, :, g], bias_g[:, g], window, dil)
        outs.append(o)
        lses.append(l)
        keep = min(window, q.shape[1])
        rows.append(jnp.stack([k[:, -keep:, g], v[:, -keep:, g]], axis=2).astype(out_dtype))
    return _combine(outs, lses), rows


def _attn_sample(q, k, v, bufs, bias_g, out_dtype):
    outs, lses, rows = [], [], []
    for g, (window, dil) in enumerate(ATTN_GROUPS):
        o, l = _dilated_sample(q[:, :, g], k[:, :, g], v[:, :, g], bufs[g], bias_g[:, g], window, dil)
        outs.append(o)
        lses.append(l)
        rows.append(jnp.stack([k[:, :, g], v[:, :, g]], axis=2).astype(out_dtype))
    return _combine(outs, lses), rows


def _rotary(x, pos):
    half = DK_R // 2
    ang = jnp.asarray(1.0 / (10000.0 ** np.linspace(0.0, 1.0, half)), jnp.float32)
    theta = pos.astype(jnp.float32)[:, None] * ang[None, :]
    c, s = jnp.cos(theta)[:, None, :], jnp.sin(theta)[:, None, :]
    x1, x2 = x[..., :half], x[..., half:]
    return jnp.concatenate([x1 * c - x2 * s, x1 * s + x2 * c], axis=-1)


def _retention_chunk(r, q, k, v, log_g):
    n = np.arange(q.shape[1])
    diff = n[:, None] - n[None, :]
    decay = jnp.where(jnp.asarray(diff >= 0)[None],
                      jnp.exp(jnp.asarray(np.maximum(diff, 0), jnp.float32)[None] * log_g[:, None, None]), 0.0)
    s = jnp.einsum('blhd,bmhd->bhlm', q, k) * decay[None]
    inner = jnp.einsum('bhlm,bmhe->blhe', s, v)
    xi = jnp.exp(jnp.asarray(n + 1, jnp.float32)[:, None] * log_g[None, :])
    cross = jnp.einsum('blhd,bhde->blhe', q, r) * xi[None, :, :, None]
    zeta = jnp.exp(jnp.asarray(n[::-1], jnp.float32)[:, None] * log_g[None, :])
    r_new = (jnp.exp(q.shape[1] * log_g)[None, :, None, None] * r
             + jnp.einsum('blhd,blhe->bhde', k * zeta[None, :, :, None], v))
    return r_new, inner + cross


def _retention_prompt(q, k, v, log_g):
    b, s = q.shape[:2]
    nc = s // RET_CHUNK

    def chunks(t):
        return jnp.moveaxis(t.reshape((b, nc, RET_CHUNK) + t.shape[2:]), 1, 0)

    r0 = jnp.zeros((b, HR, DK_R, DV_R), jnp.float32)

    def step(r, xs):
        return _retention_chunk(r, xs[0], xs[1], xs[2], log_g)

    r, o = lax.scan(step, r0, (chunks(q), chunks(k), chunks(v)))
    return jnp.moveaxis(o, 0, 1).reshape(b, s, HR, DV_R), r


def _retention_sample(q, k, v, r0, log_g):
    r, o = _retention_chunk(r0, q, k, v, log_g)
    return o, r


def _peer(h, w_q, sub_keys, u, v):
    lead = h.shape[:-1]
    t = h.reshape(-1, D_MODEL)
    n = t.shape[0]
    n_pad = -(-n // PEER_BLK) * PEER_BLK
    t = jnp.pad(t, ((0, n_pad - n), (0, 0)))
    half = PEER_DQ // 2
    sk = sub_keys.astype(jnp.float32)

    def block(tb):
        q = (tb @ w_q).reshape(PEER_BLK, PEER_HEADS, 2, half).astype(jnp.float32)
        sc = jnp.einsum('nhpc,pkc->nhpk', q, sk)
        sv, si = lax.top_k(sc, PEER_TOPK)
        cand = sv[:, :, 0, :, None] + sv[:, :, 1, None, :]
        cidx = si[:, :, 0, :, None] * N_KEYS + si[:, :, 1, None, :]
        fv, fi = lax.top_k(cand.reshape(PEER_BLK, PEER_HEADS, PEER_TOPK * PEER_TOPK), PEER_TOPK)
        eidx = jnp.take_along_axis(cidx.reshape(PEER_BLK, PEER_HEADS, -1), fi, axis=-1)
        gate = jax.nn.softmax(fv, axis=-1)
        act = jax.nn.gelu(jnp.einsum('nd,nhkd->nhk', tb, u[eidx]).astype(jnp.float32), approximate=False)
        coef = (gate * act).astype(v.dtype)
        return jnp.einsum('nhk,nhkd->nd', coef, v[eidx])

    out = lax.map(block, t.reshape(-1, PEER_BLK, D_MODEL))
    return out.reshape(n_pad, D_MODEL)[:n].reshape(lead + (D_MODEL,))


def _layer(x, pos, attn_fn, ret_fn, norm1, w_in, b_gate, qn, kn, w_oa, w_or, w_out, norm2, pwq, psk, pu, pv):
    dt = x.dtype
    lead = x.shape[:2]
    h = _rms(x, norm1).astype(dt)
    z = h @ w_in
    idx = np.cumsum(SPLITS)[:-1].tolist()
    qa, ka, va, qr, kr, vr, gr, gt = jnp.split(z, idx, axis=-1)
    ga = (N_GROUPS, HEADS_PER_GROUP, HD_A)
    qa = _rms(qa.reshape(lead + ga), qn[:, None, :]) * (HD_A ** -0.5)
    ka = _rms(ka.reshape(lead + ga), kn[:, None, :])
    va = va.reshape(lead + ga).astype(jnp.float32)
    a, kv_rows = attn_fn(qa, ka, va)
    qr = _rotary(qr.reshape(lead + (HR, DK_R)).astype(jnp.float32), pos)
    kr = _rotary(kr.reshape(lead + (HR, DK_R)).astype(jnp.float32), pos) * (DK_R ** -0.5)
    vr = vr.reshape(lead + (HR, DV_R)).astype(jnp.float32)
    o, r_new = ret_fn(qr, kr, vr)
    o = _rms(o) * jax.nn.silu(gr.reshape(lead + (HR, DV_R)).astype(jnp.float32))
    br = o.reshape(lead + (RV_W,)).astype(dt) @ w_or
    ba = a.reshape(lead + (ATTN_OUT_W,)).astype(dt) @ w_oa
    g_a, g_b = jnp.split(gt + b_gate, 2, axis=-1)
    mixed = jax.nn.sigmoid(g_a) * ba + jax.nn.sigmoid(g_b) * br
    x = x + (mixed @ w_out).astype(dt)
    x = x + _peer(_rms(x, norm2).astype(dt), pwq, psk, pu, pv).astype(dt)
    return x, kv_rows, r_new


def setup_inputs(seed: int = 0) -> dict:
    key = jax.random.key(seed)
    ks = jax.random.split(key, 24)
    f32 = jnp.float32

    def nrm(k, shape, scale):
        return jax.random.normal(k, shape, f32) * scale

    past = [min(w, PAST_LEN) for w, _ in ATTN_GROUPS]
    kvs = (HEADS_PER_GROUP, HD_A)
    return {
        'x_prompt': nrm(ks[0], (BATCH, SEQ, D_MODEL), 1.0),
        'x_sample': nrm(ks[1], (DEC_BATCH, DEC_SEQ, D_MODEL), 1.0),
        'cache_kv_d1': nrm(ks[2], (DEPTH, DEC_BATCH, past[0], 2) + kvs, 1.0),
        'cache_kv_d4': nrm(ks[3], (DEPTH, DEC_BATCH, past[1], 2) + kvs, 1.0),
        'cache_kv_d16': nrm(ks[4], (DEPTH, DEC_BATCH, past[2], 2) + kvs, 1.0),
        'state_ret': nrm(ks[5], (DEPTH, DEC_BATCH, HR, DK_R, DV_R), 0.1),
        'rel_bias': nrm(ks[6], (N_BUCKETS, N_GROUPS * HEADS_PER_GROUP), 0.5),
        'norm1_w': 1.0 + nrm(ks[7], (DEPTH, D_MODEL), 0.02),
        'w_in': nrm(ks[8], (DEPTH, D_MODEL, IN_COLS), D_MODEL ** -0.5),
        'b_gate': nrm(ks[9], (DEPTH, 2 * D_MODEL), 0.1),
        'q_norm_w': 1.0 + nrm(ks[10], (DEPTH, N_GROUPS, HD_A), 0.02),
        'k_norm_w': 1.0 + nrm(ks[11], (DEPTH, N_GROUPS, HD_A), 0.02),
        'w_o_attn': nrm(ks[12], (DEPTH, ATTN_OUT_W, D_MODEL), ATTN_OUT_W ** -0.5),
        'w_o_ret': nrm(ks[13], (DEPTH, RV_W, D_MODEL), RV_W ** -0.5),
        'w_out': nrm(ks[14], (DEPTH, D_MODEL, D_MODEL), D_MODEL ** -0.5),
        'norm2_w': 1.0 + nrm(ks[15], (DEPTH, D_MODEL), 0.02),
        'peer_w_q': nrm(ks[16], (DEPTH, D_MODEL, PEER_HEADS * PEER_DQ), D_MODEL ** -0.5),
        'peer_sub_keys': nrm(ks[17], (DEPTH, 2, N_KEYS, PEER_DQ // 2), (PEER_DQ // 2) ** -0.5),
        'peer_u': nrm(ks[18], (DEPTH, N_EXPERTS, D_MODEL), D_MODEL ** -0.5),
        'peer_v': nrm(ks[19], (DEPTH, N_EXPERTS, D_MODEL), 0.1),
    }


def reference(x_prompt, x_sample, cache_kv_d1, cache_kv_d4, cache_kv_d16, state_ret, rel_bias,
              norm1_w, w_in, b_gate, q_norm_w, k_norm_w, w_o_attn, w_o_ret, w_out, norm2_w,
              peer_w_q, peer_sub_keys, peer_u, peer_v):
    dt = x_prompt.dtype
    bias_g = rel_bias.reshape(N_BUCKETS, N_GROUPS, HEADS_PER_GROUP).astype(jnp.float32)
    log_g = jnp.asarray(np.log(1.0 - 2.0 ** (-5.0 - np.arange(HR))), jnp.float32)
    pos_p = jnp.arange(x_prompt.shape[1], dtype=jnp.int32)
    pos_s = PAST_LEN + jnp.arange(x_sample.shape[1], dtype=jnp.int32)
    caches = (cache_kv_d1, cache_kv_d4, cache_kv_d16)
    xp, xs = x_prompt, x_sample
    kv_p = [[], [], []]
    kv_s = [[], [], []]
    ret_p, ret_s = [], []
    for l in range(DEPTH):
        lw = (norm1_w[l], w_in[l], b_gate[l], q_norm_w[l], k_norm_w[l], w_o_attn[l], w_o_ret[l],
              w_out[l], norm2_w[l], peer_w_q[l], peer_sub_keys[l], peer_u[l], peer_v[l])
        xp, rows, r = _layer(xp, pos_p,
                             lambda q, k, v: _attn_prompt(q, k, v, bias_g, dt),
                             lambda q, k, v: _retention_prompt(q, k, v, log_g), *lw)
        for g in range(N_GROUPS):
            kv_p[g].append(rows[g])
        ret_p.append(r.astype(dt))
        bufs = tuple(c[l] for c in caches)
        r0 = state_ret[l].astype(jnp.float32)
        xs, rows, r = _layer(xs, pos_s,
                             lambda q, k, v: _attn_sample(q, k, v, bufs, bias_g, dt),
                             lambda q, k, v: _retention_sample(q, k, v, r0, log_g), *lw)
        for g in range(N_GROUPS):
            kv_s[g].append(rows[g])
        ret_s.append(r.astype(dt))
    kv_d1_prompt = jnp.stack(kv_p[0])
    kv_d4_prompt = jnp.stack(kv_p[1])
    kv_d16_prompt = jnp.stack(kv_p[2])
    ret_prompt = jnp.stack(ret_p)
    kv_d1_sample = jnp.stack(kv_s[0])
    kv_d4_sample = jnp.stack(kv_s[1])
    kv_d16_sample = jnp.stack(kv_s[2])
    ret_sample = jnp.stack(ret_s)
    return (xp, xs, kv_d1_prompt, kv_d4_prompt, kv_d16_prompt, ret_prompt, kv_d1_sample, kv_d4_sample, kv_d16_sample, ret_sample)
```

```python
import functools

import numpy as np
import jax
import jax.numpy as jnp
from jax import lax
from jax.experimental import pallas as pl
from jax.experimental.pallas import tpu as pltpu

F32 = jnp.float32
BF16 = jnp.bfloat16

D_MODEL = 1024
PAST_LEN = 16384
ATTN_GROUPS = ((128, 1), (512, 4), (2048, 16))
N_GROUPS = 3
HEADS = 4
HD_A = 128
ATTN_BLK = 128
N_BUCKETS = 32
REL_MAX_DIST = 2048
HR = 4
DK_R = 128
DV_R = 256
RET_CHUNK = 128
N_KEYS = 128
N_EXPERTS = N_KEYS * N_KEYS
PEER_HEADS = 8
PEER_TOPK = 16
EPS = 1e-6
NEG = -1e30

QA_W = N_GROUPS * HEADS * HD_A
RQ_W = HR * DK_R
RV_W = HR * DV_R
IN_COLS = 3 * QA_W + 2 * RQ_W + 2 * RV_W + 2 * D_MODEL
COL_TILE = 512
N_COL_TILES = IN_COLS // COL_TILE
CT_K = QA_W // COL_TILE
CT_V = 2 * QA_W // COL_TILE
CT_QR = 3 * QA_W // COL_TILE
CT_KR = CT_QR + 1
CT_VR = CT_KR + 1
CT_GR = CT_VR + 2
CT_GATE = CT_GR + 2

LANES = 128
VMEM_LIMIT = 48 * 1024 * 1024

PEER_TOK_LANES = 128
PEER_EXPERT_BLK = 1024


def _dot(a, b, precision=None):
    return jnp.dot(a, b, preferred_element_type=F32, precision=precision)


def _dot_nt(a, b, precision=None):
    return lax.dot_general(a, b, (((1,), (1,)), ((), ())), preferred_element_type=F32, precision=precision)


def _cparams(sem):
    return pltpu.CompilerParams(dimension_semantics=sem, vmem_limit_bytes=VMEM_LIMIT)


def _inproj_kernel(x_ref, n1_ref, w_ref, colw_ref, cos_ref, sin_ref, z_ref, hn_ref):
    j = pl.program_id(1)

    @pl.when(j == 0)
    def _():
        x = x_ref[...]
        ms = jnp.mean(x * x, axis=-1, keepdims=True)
        hn_ref[...] = (x * lax.rsqrt(ms + EPS) * n1_ref[...]).astype(BF16)

    y = _dot(hn_ref[...], w_ref[...])
    cw = colw_ref[...]

    @pl.when(j < CT_V)
    def _():
        for h in range(HEADS):
            sl = slice(h * HD_A, (h + 1) * HD_A)
            yh = y[:, sl]
            ms = jnp.mean(yh * yh, axis=-1, keepdims=True)
            z_ref[:, sl] = yh * lax.rsqrt(ms + EPS) * cw[:, sl]

    @pl.when((j == CT_QR) | (j == CT_KR))
    def _():
        c = cos_ref[...]
        s = sin_ref[...]
        for h in range(HR):
            sl = slice(h * DK_R, (h + 1) * DK_R)
            yh = y[:, sl]
            z_ref[:, sl] = (yh * c + pltpu.roll(yh, DK_R // 2, 1) * s) * cw[:, sl]

    @pl.when(((j >= CT_V) & (j < CT_QR)) | (j >= CT_VR))
    def _():
        z_ref[...] = y


def _inproj(x, n1, w_bf, colw, cos_t, sin_t, tm, pos_blocks):
    t = x.shape[0]
    return pl.pallas_call(
        _inproj_kernel,
        grid=(t // tm, N_COL_TILES),
        in_specs=[
            pl.BlockSpec((tm, D_MODEL), lambda i, j: (i, 0)),
            pl.BlockSpec((1, D_MODEL), lambda i, j: (0, 0)),
            pl.BlockSpec((D_MODEL, COL_TILE), lambda i, j: (0, j)),
            pl.BlockSpec((None, 1, COL_TILE), lambda i, j: (j, 0, 0)),
            pl.BlockSpec((tm, DK_R), lambda i, j: (i % pos_blocks, 0)),
            pl.BlockSpec((tm, DK_R), lambda i, j: (i % pos_blocks, 0)),
        ],
        out_specs=pl.BlockSpec((tm, COL_TILE), lambda i, j: (i, j)),
        out_shape=jax.ShapeDtypeStruct((t, IN_COLS), F32),
        scratch_shapes=[pltpu.VMEM((tm, D_MODEL), BF16)],
        compiler_params=_cparams(("parallel", "arbitrary")),
        name="inproj",
    )(x, n1, w_bf, colw, cos_t, sin_t)


def _attn_kernel(q_ref, kp_ref, kc_ref, vp_ref, vc_ref, bias_ref, o_ref, l_ref):
    i = pl.program_id(2)
    qi = lax.broadcasted_iota(jnp.int32, (ATTN_BLK, ATTN_BLK), 0)
    ki = lax.broadcasted_iota(jnp.int32, (ATTN_BLK, ATTN_BLK), 1)
    mask_p = jnp.logical_and(ki >= qi, i > 0)
    mask_c = ki <= qi
    for h in range(HEADS):
        sl = slice(h * HD_A, (h + 1) * HD_A)
        q = q_ref[:, sl].astype(BF16)
        sp = _dot_nt(q, kp_ref[:, sl].astype(BF16)) + bias_ref[h, :, :ATTN_BLK]
        sc = _dot_nt(q, kc_ref[:, sl].astype(BF16)) + bias_ref[h, :, ATTN_BLK:]
        sp = jnp.where(mask_p, sp, NEG)
        sc = jnp.where(mask_c, sc, NEG)
        m = jnp.maximum(jnp.max(sp, axis=-1, keepdims=True), jnp.max(sc, axis=-1, keepdims=True))
        pp = jnp.exp(sp - m)
        pc = jnp.exp(sc - m)
        den = jnp.sum(pp, axis=-1, keepdims=True) + jnp.sum(pc, axis=-1, keepdims=True)
        acc = _dot(pp.astype(BF16), vp_ref[:, sl].astype(BF16)) + _dot(pc.astype(BF16), vc_ref[:, sl].astype(BF16))
        o_ref[:, sl] = acc / den
        l_ref[:, sl] = jnp.broadcast_to(m + jnp.log(den), (ATTN_BLK, HD_A))


def _attn_prompt(z3, g, d, bias, b, s):
    ls = s // d
    nb = ls // ATTN_BLK
    zv = z3.reshape(b, ls, d * IN_COLS)
    gw = HEADS * HD_A

    def spec(col_tile, prev):
        if prev:
            return pl.BlockSpec((None, ATTN_BLK, gw), lambda bb, r, i: (bb, jnp.maximum(i - 1, 0), r * N_COL_TILES + col_tile))
        return pl.BlockSpec((None, ATTN_BLK, gw), lambda bb, r, i: (bb, i, r * N_COL_TILES + col_tile))

    out_spec = pl.BlockSpec((None, ATTN_BLK, gw), lambda bb, r, i: (bb, i, r))
    o, l = pl.pallas_call(
        _attn_kernel,
        grid=(b, d, nb),
        in_specs=[
            spec(g, False),
            spec(CT_K + g, True),
            spec(CT_K + g, False),
            spec(CT_V + g, True),
            spec(CT_V + g, False),
            pl.BlockSpec((HEADS, ATTN_BLK, 2 * ATTN_BLK), lambda bb, r, i: (0, 0, 0)),
        ],
        out_specs=[out_spec, out_spec],
        out_shape=[jax.ShapeDtypeStruct((b, ls, d * gw), F32)] * 2,
        compiler_params=_cparams(("parallel", "parallel", "arbitrary")),
        name=f"attn_d{d}",
    )(zv, zv, zv, zv, zv, bias)
    return o.reshape(b * s, gw), l.reshape(b * s, gw)


def _attn_sample_kernel(q_ref, k_ref, v_ref, c0_ref, c1_ref, c2_ref, brow_ref, bself_ref, a_ref):
    caches = (c0_ref, c1_ref, c2_ref)
    gw = HEADS * HD_A
    for h in range(HEADS):
        outs, lses = [], []
        for g in range(N_GROUPS):
            col = g * gw + h * HD_A
            q = q_ref[:, col:col + HD_A]
            kn = k_ref[:, col:col + HD_A]
            vn = v_ref[:, col:col + HD_A]
            kc = caches[g][:, h * HD_A:(h + 1) * HD_A]
            vc = caches[g][:, gw + h * HD_A:gw + (h + 1) * HD_A]
            sc = jnp.sum(kc * q, axis=-1, keepdims=True) + brow_ref[g, :, h:h + 1]
            s0 = jnp.sum(kn * q, axis=-1, keepdims=True) + bself_ref[g, :, h:h + 1][0:1]
            m = jnp.maximum(jnp.max(sc, axis=0, keepdims=True), s0)
            e = jnp.exp(sc - m)
            e0 = jnp.exp(s0 - m)
            den = jnp.sum(e, axis=0, keepdims=True) + e0
            outs.append((jnp.sum(e * vc, axis=0, keepdims=True) + e0 * vn) / den)
            lses.append(m + jnp.log(den))
        mm = jnp.maximum(jnp.maximum(lses[0], lses[1]), lses[2])
        ws = [jnp.exp(l - mm) for l in lses]
        wsum = ws[0] + ws[1] + ws[2]
        a_ref[:, h * HD_A:(h + 1) * HD_A] = (ws[0] * outs[0] + ws[1] * outs[1] + ws[2] * outs[2]) / wsum


def _attn_sample(qa, ka, va, caches, brow, bself):
    db = qa.shape[0]
    gw = HEADS * HD_A
    cviews, cspecs = [], []
    for c, (window, d) in zip(caches, ATTN_GROUPS):
        assert c.shape[1] == window, "sample attention expects a full window of cached rows"
        cviews.append(c.reshape(db, window // d, d * 2 * gw))
        cspecs.append(pl.BlockSpec((None, ATTN_BLK, 2 * gw), lambda b: (b, 0, 0)))
    tok = pl.BlockSpec((None, 1, QA_W), lambda b: (b, 0, 0))
    return pl.pallas_call(
        _attn_sample_kernel,
        grid=(db,),
        in_specs=[tok, tok, tok] + cspecs + [
            pl.BlockSpec((N_GROUPS, ATTN_BLK, HEADS), lambda b: (0, 0, 0)),
            pl.BlockSpec((N_GROUPS, 8, HEADS), lambda b: (0, 0, 0)),
        ],
        out_specs=pl.BlockSpec((None, 1, gw), lambda b: (b, 0, 0)),
        out_shape=jax.ShapeDtypeStruct((db, 1, gw), F32),
        compiler_params=_cparams(("parallel",)),
        name="attn_sample",
    )(qa, ka, va, *cviews, brow, bself)


def _ret_epilogue(o, gate):
    on = o * lax.rsqrt(jnp.mean(o * o, axis=-1, keepdims=True) + EPS)
    return on * (gate / (1.0 + jnp.exp(-gate)))


def _ret_prompt_kernel(q_ref, k_ref, va_ref, vb_ref, ga_ref, gb_ref, dec_ref, xi_ref, zeta_ref, o_ref, rout_ref, r_scr):
    c = pl.program_id(1)
    heads_per_tile = COL_TILE // DV_R

    @pl.when(c == 0)
    def _():
        r_scr[...] = jnp.zeros_like(r_scr)

    for h in range(HR):
        qs = slice(h * DK_R, (h + 1) * DK_R)
        vs = slice(h * DV_R, (h + 1) * DV_R)
        ts = slice((h % heads_per_tile) * DV_R, (h % heads_per_tile + 1) * DV_R)
        v_ref, g_ref = ((va_ref, ga_ref), (vb_ref, gb_ref))[h // heads_per_tile]
        q = q_ref[:, qs]
        k = k_ref[:, qs]
        qb = q.astype(BF16)
        vb = v_ref[:, ts].astype(BF16)
        r = r_scr[h]
        xi = xi_ref[h]
        s = _dot_nt(qb, k.astype(BF16)) * dec_ref[h]
        o = _dot(s.astype(BF16), vb) + _dot(qb, r.astype(BF16)) * xi
        kz = (k * zeta_ref[h]).T
        r_scr[h] = xi[RET_CHUNK - 1:RET_CHUNK, :] * r + _dot(kz.astype(BF16), vb)
        o_ref[:, vs] = _ret_epilogue(o, g_ref[:, ts])

    @pl.when(c == pl.num_programs(1) - 1)
    def _():
        rout_ref[...] = r_scr[...]


def _ret_prompt(z3, dec, xi, zeta, b, s):
    nc = s // RET_CHUNK
    qk_w = HR * DK_R
    tab = lambda w: pl.BlockSpec((HR, RET_CHUNK, w), lambda bb, c: (0, 0, 0))
    o, r = pl.pallas_call(
        _ret_prompt_kernel,
        grid=(b, nc),
        in_specs=[
            pl.BlockSpec((None, RET_CHUNK, qk_w), lambda bb, c: (bb, c, CT_QR)),
            pl.BlockSpec((None, RET_CHUNK, qk_w), lambda bb, c: (bb, c, CT_KR)),
            pl.BlockSpec((None, RET_CHUNK, COL_TILE), lambda bb, c: (bb, c, CT_VR)),
            pl.BlockSpec((None, RET_CHUNK, COL_TILE), lambda bb, c: (bb, c, CT_VR + 1)),
            pl.BlockSpec((None, RET_CHUNK, COL_TILE), lambda bb, c: (bb, c, CT_GR)),
            pl.BlockSpec((None, RET_CHUNK, COL_TILE), lambda bb, c: (bb, c, CT_GR + 1)),
            tab(RET_CHUNK), tab(DV_R), tab(DK_R),
        ],
        out_specs=[
            pl.BlockSpec((None, RET_CHUNK, RV_W), lambda bb, c: (bb, c, 0)),
            pl.BlockSpec((None, HR, DK_R, DV_R), lambda bb, c: (bb, 0, 0, 0)),
        ],
        out_shape=[jax.ShapeDtypeStruct((b, s, RV_W), F32), jax.ShapeDtypeStruct((b, HR, DK_R, DV_R), F32)],
        scratch_shapes=[pltpu.VMEM((HR, DK_R, DV_R), F32)],
        compiler_params=_cparams(("parallel", "arbitrary")),
        name="ret_prompt",
    )(z3, z3, z3, z3, z3, z3, dec, xi, zeta)
    return o.reshape(b * s, RV_W), r


def _ret_sample_kernel(q_ref, k_ref, v_ref, g_ref, r0_ref, gam_ref, o_ref, rout_ref):
    eye = (lax.broadcasted_iota(jnp.int32, (DK_R, DK_R), 0) == lax.broadcasted_iota(jnp.int32, (DK_R, DK_R), 1))
    for h in range(HR):
        qs = slice(h * DK_R, (h + 1) * DK_R)
        vs = slice(h * DV_R, (h + 1) * DV_R)
        q = q_ref[:, qs]
        k = k_ref[:, qs]
        v = v_ref[:, vs]
        r = r0_ref[h]
        gam = gam_ref[h]
        qcol = jnp.sum(jnp.where(eye, jnp.broadcast_to(q, (DK_R, DK_R)), 0.0), axis=-1, keepdims=True)
        kcol = jnp.sum(jnp.where(eye, jnp.broadcast_to(k, (DK_R, DK_R)), 0.0), axis=-1, keepdims=True)
        qk = jnp.sum(q * k, axis=-1, keepdims=True)
        o = qk * v + jnp.sum(qcol * r, axis=0, keepdims=True) * gam
        rout_ref[h] = gam * r + kcol * v
        o_ref[:, vs] = _ret_epilogue(o, g_ref[:, vs])


def _ret_sample(qr, kr, vr, gr, r0, gam):
    db = qr.shape[0]
    qk_w = HR * DK_R
    return pl.pallas_call(
        _ret_sample_kernel,
        grid=(db,),
        in_specs=[
            pl.BlockSpec((None, 1, qk_w), lambda b: (b, 0, 0)),
            pl.BlockSpec((None, 1, qk_w), lambda b: (b, 0, 0)),
            pl.BlockSpec((None, 1, RV_W), lambda b: (b, 0, 0)),
            pl.BlockSpec((None, 1, RV_W), lambda b: (b, 0, 0)),
            pl.BlockSpec((None, HR, DK_R, DV_R), lambda b: (b, 0, 0, 0)),
            pl.BlockSpec((HR, 1, DV_R), lambda b: (0, 0, 0)),
        ],
        out_specs=[
            pl.BlockSpec((None, 1, RV_W), lambda b: (b, 0, 0)),
            pl.BlockSpec((None, HR, DK_R, DV_R), lambda b: (b, 0, 0, 0)),
        ],
        out_shape=[jax.ShapeDtypeStruct((db, 1, RV_W), F32), jax.ShapeDtypeStruct((db, HR, DK_R, DV_R), F32)],
        compiler_params=_cparams(("parallel",)),
        name="ret_sample",
    )(qr, kr, vr, gr, r0, gam)


def _merge_kernel(*refs, combined):
    if combined:
        a_ref = refs[0]
        rest = refs[1:]
        a = a_ref[...]
    else:
        o0, o1, o2, l0, l1, l2 = refs[:6]
        rest = refs[6:]
        la, lb, lc = l0[...], l1[...], l2[...]
        m = jnp.maximum(jnp.maximum(la, lb), lc)
        ea, eb, ec = jnp.exp(la - m), jnp.exp(lb - m), jnp.exp(lc - m)
        a = (ea * o0[...] + eb * o1[...] + ec * o2[...]) / (ea + eb + ec)
    (ro_ref, ga0, ga1, gb0, gb1, x_ref, woa_ref, wor_ref, wout_ref, bg_ref, n2_ref, x1_ref, h2_ref) = rest
    ba = _dot(a.astype(BF16), woa_ref[...])
    br = _dot(ro_ref[...].astype(BF16), wor_ref[...])
    ga = jnp.concatenate([ga0[...], ga1[...]], axis=-1) + bg_ref[:, :D_MODEL]
    gb = jnp.concatenate([gb0[...], gb1[...]], axis=-1) + bg_ref[:, D_MODEL:]
    mixed = ba / (1.0 + jnp.exp(-ga)) + br / (1.0 + jnp.exp(-gb))
    x1 = x_ref[...] + _dot(mixed.astype(BF16), wout_ref[...])
    x1_ref[...] = x1
    h2_ref[...] = x1 * lax.rsqrt(jnp.mean(x1 * x1, axis=-1, keepdims=True) + EPS) * n2_ref[...]


def _merge(attn_in, ro, z, x, woa, wor, wout, bg, n2, tm):
    t = x.shape[0]
    gw = HEADS * HD_A
    combined = len(attn_in) == 1
    row = lambda w, c=0: pl.BlockSpec((tm, w), lambda i, c=c: (i, c))
    full = lambda a: pl.BlockSpec(a.shape, lambda i: (0, 0))
    in_specs = [row(gw)] * len(attn_in) + [
        row(RV_W),
        row(COL_TILE, CT_GATE), row(COL_TILE, CT_GATE + 1), row(COL_TILE, CT_GATE + 2), row(COL_TILE, CT_GATE + 3),
        row(D_MODEL),
        full(woa), full(wor), full(wout), full(bg), full(n2),
    ]
    return pl.pallas_call(
        functools.partial(_merge_kernel, combined=combined),
        grid=(t // tm,),
        in_specs=in_specs,
        out_specs=[row(D_MODEL), row(D_MODEL)],
        out_shape=[jax.ShapeDtypeStruct((t, D_MODEL), F32)] * 2,
        compiler_params=_cparams(("parallel",)),
        name="merge",
    )(*attn_in, ro, z, z, z, z, x, woa, wor, wout, bg, n2)


def _topk_rank(s):
    key = lax.broadcasted_iota(jnp.int32, s.shape, 0).astype(F32)
    rank = jnp.full(s.shape, float(PEER_TOPK), F32)
    vals = []
    for k in range(PEER_TOPK):
        m = jnp.max(s, axis=0, keepdims=True)
        first = jnp.min(jnp.where(s == m, key, float(N_KEYS)), axis=0, keepdims=True)
        hit = key == first
        rank = jnp.where(hit, float(k), rank)
        s = jnp.where(hit, -jnp.inf, s)
        vals.append(m)
    return rank, vals


def _peer_select_kernel(h_ref, wq_ref, sk_ref, sel_ref):
    hblk = h_ref[...]
    ntok = hblk.shape[0]
    hi = lax.Precision.HIGHEST
    for hd in range(PEER_HEADS):
        sc = []
        for p in range(2):
            row0 = (hd * 2 + p) * N_KEYS
            q_t = _dot_nt(wq_ref[row0:row0 + N_KEYS, :], hblk, hi)
            sc.append(_dot(sk_ref[p], q_t, hi))
        rank1, sv1 = _topk_rank(sc[0])
        rank2, sv2 = _topk_rank(sc[1])
        sv2_all = jnp.concatenate(sv2, axis=0)
        cand = jnp.concatenate([sv1[a] + sv2_all for a in range(PEER_TOPK)], axis=0)
        flat = lax.broadcasted_iota(jnp.int32, cand.shape, 0).astype(F32)
        work = cand
        chosen = jnp.zeros(cand.shape, F32)
        for _ in range(PEER_TOPK):
            m = jnp.max(work, axis=0, keepdims=True)
            first = jnp.min(jnp.where(work == m, flat, float(PEER_TOPK * PEER_TOPK)), axis=0, keepdims=True)
            hit = flat == first
            chosen = jnp.where(hit, 1.0, chosen)
            work = jnp.where(hit, -jnp.inf, work)
        top = sv1[0] + sv2[0]
        zsum = jnp.sum(chosen * jnp.exp(cand - top), axis=0, keepdims=True)
        cnt1 = jnp.zeros((N_KEYS, ntok), F32)
        for a in range(PEER_TOPK):
            cnt_a = jnp.sum(chosen[a * PEER_TOPK:(a + 1) * PEER_TOPK], axis=0, keepdims=True)
            cnt1 = jnp.where(rank1 == float(a), cnt_a, cnt1)
        sel_ref[hd, 0] = cnt1
        sel_ref[hd, 1] = jnp.exp(sc[0] - sv1[0]) / zsum
        sel_ref[hd, 2] = rank2
        sel_ref[hd, 3] = jnp.exp(sc[1] - sv2[0])


def _peer_select(h2, wq_t, sk):
    t = h2.shape[0]
    tl = PEER_TOK_LANES
    return pl.pallas_call(
        _peer_select_kernel,
        grid=(t // tl,),
        in_specs=[
            pl.BlockSpec((tl, D_MODEL), lambda i: (i, 0)),
            pl.BlockSpec(wq_t.shape, lambda i: (0, 0)),
            pl.BlockSpec(sk.shape, lambda i: (0, 0, 0)),
        ],
        out_specs=pl.BlockSpec((None, PEER_HEADS, 4, N_KEYS, tl), lambda i: (i, 0, 0, 0, 0)),
        out_shape=jax.ShapeDtypeStruct((t // tl, PEER_HEADS, 4, N_KEYS, tl), F32),
        compiler_params=_cparams(("parallel",)),
        name="peer_select",
    )(h2, wq_t, sk)


def _gelu(x):
    return 0.5 * x * (1.0 + lax.erf(x * np.float32(2.0 ** -0.5)))


def _peer_dense_kernel(h_ref, sel_ref, u_ref, vt_ref, x_ref, out_ref, hb_ref, act_ref, g_ref, acc_ref, *, nsub):
    e = pl.program_id(1)
    keys_per_blk = PEER_EXPERT_BLK // N_KEYS

    @pl.when(e == 0)
    def _():
        hb_ref[...] = h_ref[...].astype(BF16)
        acc_ref[...] = jnp.zeros_like(acc_ref)

    act_ref[...] = _gelu(_dot_nt(u_ref[...], hb_ref[...]))

    for t in range(nsub):
        lanes = slice(t * PEER_TOK_LANES, (t + 1) * PEER_TOK_LANES)

        def body(ii, carry):
            i = e * keys_per_blk + ii
            w = jnp.zeros((N_KEYS, PEER_TOK_LANES), F32)
            for hd in range(PEER_HEADS):
                cnt_row = sel_ref[t, hd, 0, pl.ds(i, 1), :]
                e1_row = sel_ref[t, hd, 1, pl.ds(i, 1), :]
                w = w + jnp.where(sel_ref[t, hd, 2] < cnt_row, e1_row * sel_ref[t, hd, 3], 0.0)
            rows = pl.ds(pl.multiple_of(ii * N_KEYS, N_KEYS), N_KEYS)
            g_ref[rows, lanes] = (w * act_ref[rows, lanes]).astype(BF16)
            return carry

        lax.fori_loop(0, keys_per_blk, body, 0)

    acc_ref[...] += _dot(vt_ref[...], g_ref[...])

    @pl.when(e == pl.num_programs(1) - 1)
    def _():
        out_ref[...] = x_ref[...] + acc_ref[...].T


def _peer_dense(h2, sel, u_bf, vt_bf, x1, tb):
    t = h2.shape[0]
    nsub = tb // PEER_TOK_LANES
    eb = PEER_EXPERT_BLK
    return pl.pallas_call(
        functools.partial(_peer_dense_kernel, nsub=nsub),
        grid=(t // tb, N_EXPERTS // eb),
        in_specs=[
            pl.BlockSpec((tb, D_MODEL), lambda i, e: (i, 0)),
            pl.BlockSpec((nsub, PEER_HEADS, 4, N_KEYS, PEER_TOK_LANES), lambda i, e: (i, 0, 0, 0, 0)),
            pl.BlockSpec((eb, D_MODEL), lambda i, e: (e, 0)),
            pl.BlockSpec((D_MODEL, eb), lambda i, e: (0, e)),
            pl.BlockSpec((tb, D_MODEL), lambda i, e: (i, 0)),
        ],
        out_specs=pl.BlockSpec((tb, D_MODEL), lambda i, e: (i, 0)),
        out_shape=jax.ShapeDtypeStruct((t, D_MODEL), F32),
        scratch_shapes=[
            pltpu.VMEM((tb, D_MODEL), BF16),
            pltpu.VMEM((eb, tb), F32),
            pltpu.VMEM((eb, tb), BF16),
            pltpu.VMEM((D_MODEL, tb), F32),
        ],
        compiler_params=_cparams(("parallel", "arbitrary")),
        name="peer_dense",
    )(h2, sel, u_bf, vt_bf, x1)


def _peer(h2, x1, wq_t, sk, u_bf, vt_bf):
    t = h2.shape[0]
    tb = min(512, -(-t // PEER_TOK_LANES) * PEER_TOK_LANES)
    t_pad = -(-t // tb) * tb
    if t_pad != t:
        h2 = jnp.pad(h2, ((0, t_pad - t), (0, 0)))
        x1 = jnp.pad(x1, ((0, t_pad - t), (0, 0)))
    sel = _peer_select(h2, wq_t, sk)
    return _peer_dense(h2, sel, u_bf, vt_bf, x1, tb)[:t]


def _t5_bucket(dist):
    dist = np.asarray(dist, np.int64)
    max_exact = N_BUCKETS // 2
    ratio = np.log(np.maximum(dist, 1) / max_exact) / np.log(REL_MAX_DIST / max_exact)
    large = np.minimum(max_exact + (ratio * (N_BUCKETS - max_exact)).astype(np.int64), N_BUCKETS - 1)
    return np.where(dist < max_exact, dist, large).astype(np.int32)


def _rotary_tables(pos):
    half = DK_R // 2
    ang = jnp.asarray(1.0 / (10000.0 ** np.linspace(0.0, 1.0, half)), F32)
    theta = pos.astype(F32)[:, None] * ang[None, :]
    c, s = jnp.cos(theta), jnp.sin(theta)
    return jnp.concatenate([c, c], axis=-1), jnp.concatenate([-s, s], axis=-1)


def _retention_tables(length, log_g):
    n = np.arange(length)
    diff = n[:, None] - n[None, :]
    decay = jnp.where(jnp.asarray(diff >= 0)[None],
                      jnp.exp(jnp.asarray(np.maximum(diff, 0), F32)[None] * log_g[:, None, None]), 0.0)
    xi = jnp.exp(jnp.asarray(n + 1, F32)[None, :] * log_g[:, None])
    zeta = jnp.exp(jnp.asarray(n[::-1].copy(), F32)[None, :] * log_g[:, None])
    xi = jnp.broadcast_to(xi[:, :, None], (HR, length, DV_R))
    zeta = jnp.broadcast_to(zeta[:, :, None], (HR, length, DK_R))
    return decay, xi, zeta


def _prompt_bias(bias_g, g, d):
    steps = ATTN_GROUPS[g][0] // d
    qi = np.arange(ATTN_BLK)[:, None]
    ki = np.arange(2 * ATTN_BLK)[None, :]
    m = ATTN_BLK + qi - ki
    return jnp.transpose(bias_g[:, g][_t5_bucket(np.clip(m, 0, steps) * d)], (2, 0, 1))


def _sample_bias(bias_g):
    rows, selfs = [], []
    for g, (window, d) in enumerate(ATTN_GROUPS):
        steps = window // d
        assert steps == ATTN_BLK
        j = steps - np.arange(steps)
        rows.append(bias_g[:, g][_t5_bucket(j * d)])
        selfs.append(jnp.broadcast_to(bias_g[:, g][_t5_bucket(np.zeros(1))], (8, HEADS)))
    return jnp.stack(rows), jnp.stack(selfs)


def kernel(x_prompt, x_sample, cache_kv_d1, cache_kv_d4, cache_kv_d16, state_ret, rel_bias, norm1_w, w_in, b_gate, q_norm_w, k_norm_w, w_o_attn, w_o_ret, w_out, norm2_w, peer_w_q, peer_sub_keys, peer_u, peer_v):
    b, s, _ = x_prompt.shape
    db, ds, _ = x_sample.shape
    assert ds == 1, "the sample group is one new token per sequence"
    depth = w_in.shape[0]
    gw = HEADS * HD_A
    caches = (cache_kv_d1, cache_kv_d4, cache_kv_d16)

    bias_g = rel_bias.reshape(N_BUCKETS, N_GROUPS, HEADS).astype(F32)
    log_g = jnp.asarray(np.log(1.0 - 2.0 ** (-5.0 - np.arange(HR))), F32)
    cos_p, sin_p = _rotary_tables(jnp.arange(s, dtype=jnp.int32))
    cos_s, sin_s = _rotary_tables(jnp.full((db,), PAST_LEN, jnp.int32))
    dec_p, xi_p, zeta_p = _retention_tables(RET_CHUNK, log_g)
    gam_s = _retention_tables(1, log_g)[1]
    bias_p = [_prompt_bias(bias_g, g, d) for g, (_, d) in enumerate(ATTN_GROUPS)]
    brow_s, bself_s = _sample_bias(bias_g)

    tm_p = 512 if (b * s) % 512 == 0 else ATTN_BLK
    xp = x_prompt.reshape(b * s, D_MODEL)
    xs = x_sample.reshape(db, D_MODEL)
    kv_p = [[] for _ in range(N_GROUPS)]
    kv_s = [[] for _ in range(N_GROUPS)]
    ret_p, ret_s = [], []

    for l in range(depth):
        w_in_bf = w_in[l].astype(BF16)
        ones = jnp.ones((COL_TILE,), F32)
        colw = jnp.stack(
            [jnp.tile(q_norm_w[l, g], HEADS) * (HD_A ** -0.5) for g in range(N_GROUPS)]
            + [jnp.tile(k_norm_w[l, g], HEADS) for g in range(N_GROUPS)]
            + [ones] * (CT_KR - CT_V) + [ones * (DK_R ** -0.5)] + [ones] * (N_COL_TILES - CT_KR - 1)
        ).reshape(N_COL_TILES, 1, COL_TILE)
        n1 = norm1_w[l].reshape(1, D_MODEL)
        n2 = norm2_w[l].reshape(1, D_MODEL)
        bg = b_gate[l].reshape(1, 2 * D_MODEL)
        woa, wor, wout = w_o_attn[l].astype(BF16), w_o_ret[l].astype(BF16), w_out[l].astype(BF16)
        wq_t = peer_w_q[l].T
        sk = peer_sub_keys[l]
        u_bf = peer_u[l].astype(BF16)
        vt_bf = peer_v[l].T.astype(BF16)

        zp = _inproj(xp, n1, w_in_bf, colw, cos_p, sin_p, tm_p, s // tm_p)
        zp3 = zp.reshape(b, s, IN_COLS)
        attn_in = [None] * (2 * N_GROUPS)
        for g, (window, d) in enumerate(ATTN_GROUPS):
            attn_in[g], attn_in[N_GROUPS + g] = _attn_prompt(zp3, g, d, bias_p[g], b, s)
            keep = min(window, s)
            k_rows = zp3[:, s - keep:, QA_W + g * gw:QA_W + (g + 1) * gw].reshape(b, keep, HEADS, HD_A)
            v_rows = zp3[:, s - keep:, 2 * QA_W + g * gw:2 * QA_W + (g + 1) * gw].reshape(b, keep, HEADS, HD_A)
            kv_p[g].append(jnp.stack([k_rows, v_rows], axis=2))
        ro, r_new = _ret_prompt(zp3, dec_p, xi_p, zeta_p, b, s)
        ret_p.append(r_new)
        x1, h2 = _merge(attn_in, ro, zp, xp, woa, wor, wout, bg, n2, 256 if (b * s) % 256 == 0 else ATTN_BLK)
        xp = _peer(h2, x1, wq_t, sk, u_bf, vt_bf)

        zs = _inproj(xs, n1, w_in_bf, colw, cos_s, sin_s, db, 1)
        zs3 = zs.reshape(db, 1, IN_COLS)
        qa_s, ka_s, va_s = zs3[:, :, :QA_W], zs3[:, :, QA_W:2 * QA_W], zs3[:, :, 2 * QA_W:3 * QA_W]
        c0 = 3 * QA_W
        qr_s, kr_s = zs3[:, :, c0:c0 + RQ_W], zs3[:, :, c0 + RQ_W:c0 + 2 * RQ_W]
        c1 = c0 + 2 * RQ_W
        vr_s, gr_s = zs3[:, :, c1:c1 + RV_W], zs3[:, :, c1 + RV_W:c1 + 2 * RV_W]
        a_s = _attn_sample(qa_s, ka_s, va_s, [c[l] for c in caches], brow_s, bself_s)
        for g in range(N_GROUPS):
            k_rows = ka_s[:, :, g * gw:(g + 1) * gw].reshape(db, 1, HEADS, HD_A)
            v_rows = va_s[:, :, g * gw:(g + 1) * gw].reshape(db, 1, HEADS, HD_A)
            kv_s[g].append(jnp.stack([k_rows, v_rows], axis=2))
        ro_s, r_s = _ret_sample(qr_s, kr_s, vr_s, gr_s, state_ret[l], gam_s)
        ret_s.append(r_s)
        x1s, h2s = _merge([a_s.reshape(db, gw)], ro_s.reshape(db, RV_W), zs, xs, woa, wor, wout, bg, n2, db)
        xs = _peer(h2s, x1s, wq_t, sk, u_bf, vt_bf)

    return (xp.reshape(b, s, D_MODEL), xs.reshape(db, 1, D_MODEL),
            jnp.stack(kv_p[0]), jnp.stack(kv_p[1]), jnp.stack(kv_p[2]), jnp.stack(ret_p),
            jnp.stack(kv_s[0]), jnp.stack(kv_s[1]), jnp.stack(kv_s[2]), jnp.stack(ret_s))
```

```python
import functools

import numpy as np
import jax
import jax.numpy as jnp
from jax import lax
from jax.experimental import pallas as pl
from jax.experimental.pallas import tpu as pltpu

F32 = jnp.float32
BF16 = jnp.bfloat16

D_MODEL = 1024
PAST_LEN = 16384
ATTN_GROUPS = ((128, 1), (512, 4), (2048, 16))
N_GROUPS = 3
HEADS = 4
HD_A = 128
ATTN_BLK = 128
N_BUCKETS = 32
REL_MAX_DIST = 2048
HR = 4
DK_R = 128
DV_R = 256
RET_CHUNK = 128
N_KEYS = 128
N_EXPERTS = N_KEYS * N_KEYS
PEER_HEADS = 8
PEER_TOPK = 16
EPS = 1e-6
NEG = -1e30

QA_W = N_GROUPS * HEADS * HD_A
RQ_W = HR * DK_R
RV_W = HR * DV_R
IN_COLS = 3 * QA_W + 2 * RQ_W + 2 * RV_W + 2 * D_MODEL
COL_TILE = 512
N_COL_TILES = IN_COLS // COL_TILE
CT_K = QA_W // COL_TILE
CT_V = 2 * QA_W // COL_TILE
CT_QR = 3 * QA_W // COL_TILE
CT_KR = CT_QR + 1
CT_VR = CT_KR + 1
CT_GR = CT_VR + 2
CT_GATE = CT_GR + 2

LANES = 128
VMEM_LIMIT = 48 * 1024 * 1024

PEER_TOK_LANES = 128
PEER_EXPERT_BLK = 1024
PEER_EXPERT_CHUNK = 256
PEER_KEY_SLICE = 32


def _dot(a, b, precision=None):
    return jnp.dot(a, b, preferred_element_type=F32, precision=precision)


def _dot_nt(a, b, precision=None):
    return lax.dot_general(a, b, (((1,), (1,)), ((), ())), preferred_element_type=F32, precision=precision)


def _cparams(sem):
    return pltpu.CompilerParams(dimension_semantics=sem, vmem_limit_bytes=VMEM_LIMIT)


def _inproj_kernel(x_ref, n1_ref, w_ref, colw_ref, cos_ref, sin_ref, z_ref, hn_ref):
    j = pl.program_id(1)

    @pl.when(j == 0)
    def _():
        x = x_ref[...]
        ms = jnp.mean(x * x, axis=-1, keepdims=True)
        hn_ref[...] = (x * lax.rsqrt(ms + EPS) * n1_ref[...]).astype(BF16)

    y = _dot(hn_ref[...], w_ref[...])
    cw = colw_ref[...]

    @pl.when(j < CT_V)
    def _():
        for h in range(HEADS):
            sl = slice(h * HD_A, (h + 1) * HD_A)
            yh = y[:, sl]
            ms = jnp.mean(yh * yh, axis=-1, keepdims=True)
            z_ref[:, sl] = yh * lax.rsqrt(ms + EPS) * cw[:, sl]

    @pl.when((j == CT_QR) | (j == CT_KR))
    def _():
        c = cos_ref[...]
        s = sin_ref[...]
        for h in range(HR):
            sl = slice(h * DK_R, (h + 1) * DK_R)
            yh = y[:, sl]
            z_ref[:, sl] = (yh * c + pltpu.roll(yh, DK_R // 2, 1) * s) * cw[:, sl]

    @pl.when(((j >= CT_V) & (j < CT_QR)) | (j >= CT_VR))
    def _():
        z_ref[...] = y


def _inproj(x, n1, w_bf, colw, cos_t, sin_t, tm, pos_blocks):
    t = x.shape[0]
    return pl.pallas_call(
        _inproj_kernel,
        grid=(t // tm, N_COL_TILES),
        in_specs=[
            pl.BlockSpec((tm, D_MODEL), lambda i, j: (i, 0)),
            pl.BlockSpec((1, D_MODEL), lambda i, j: (0, 0)),
            pl.BlockSpec((D_MODEL, COL_TILE), lambda i, j: (0, j)),
            pl.BlockSpec((None, 1, COL_TILE), lambda i, j: (j, 0, 0)),
            pl.BlockSpec((tm, DK_R), lambda i, j: (i % pos_blocks, 0)),
            pl.BlockSpec((tm, DK_R), lambda i, j: (i % pos_blocks, 0)),
        ],
        out_specs=pl.BlockSpec((tm, COL_TILE), lambda i, j: (i, j)),
        out_shape=jax.ShapeDtypeStruct((t, IN_COLS), F32),
        scratch_shapes=[pltpu.VMEM((tm, D_MODEL), BF16)],
        compiler_params=_cparams(("parallel", "arbitrary")),
        name="inproj",
    )(x, n1, w_bf, colw, cos_t, sin_t)


def _attn_kernel(q_ref, kp_ref, kc_ref, vp_ref, vc_ref, bias_ref, o_ref, l_ref):
    i = pl.program_id(2)
    qi = lax.broadcasted_iota(jnp.int32, (ATTN_BLK, ATTN_BLK), 0)
    ki = lax.broadcasted_iota(jnp.int32, (ATTN_BLK, ATTN_BLK), 1)
    mask_p = jnp.logical_and(ki >= qi, i > 0)
    mask_c = ki <= qi
    for h in range(HEADS):
        sl = slice(h * HD_A, (h + 1) * HD_A)
        q = q_ref[:, sl].astype(BF16)
        sp = _dot_nt(q, kp_ref[:, sl].astype(BF16)) + bias_ref[h, :, :ATTN_BLK]
        sc = _dot_nt(q, kc_ref[:, sl].astype(BF16)) + bias_ref[h, :, ATTN_BLK:]
        sp = jnp.where(mask_p, sp, NEG)
        sc = jnp.where(mask_c, sc, NEG)
        m = jnp.maximum(jnp.max(sp, axis=-1, keepdims=True), jnp.max(sc, axis=-1, keepdims=True))
        pp = jnp.exp(sp - m)
        pc = jnp.exp(sc - m)
        den = jnp.sum(pp, axis=-1, keepdims=True) + jnp.sum(pc, axis=-1, keepdims=True)
        acc = _dot(pp.astype(BF16), vp_ref[:, sl].astype(BF16)) + _dot(pc.astype(BF16), vc_ref[:, sl].astype(BF16))
        o_ref[:, sl] = acc / den
        l_ref[:, sl] = jnp.broadcast_to(m + jnp.log(den), (ATTN_BLK, HD_A))


def _attn_prompt(z3, g, d, bias, b, s):
    ls = s // d
    nb = ls // ATTN_BLK
    zv = z3.reshape(b, ls, d * IN_COLS)
    gw = HEADS * HD_A

    def spec(col_tile, prev):
        if prev:
            return pl.BlockSpec((None, ATTN_BLK, gw), lambda bb, r, i: (bb, jnp.maximum(i - 1, 0), r * N_COL_TILES + col_tile))
        return pl.BlockSpec((None, ATTN_BLK, gw), lambda bb, r, i: (bb, i, r * N_COL_TILES + col_tile))

    out_spec = pl.BlockSpec((None, ATTN_BLK, gw), lambda bb, r, i: (bb, i, r))
    o, l = pl.pallas_call(
        _attn_kernel,
        grid=(b, d, nb),
        in_specs=[
            spec(g, False),
            spec(CT_K + g, True),
            spec(CT_K + g, False),
            spec(CT_V + g, True),
            spec(CT_V + g, False),
            pl.BlockSpec((HEADS, ATTN_BLK, 2 * ATTN_BLK), lambda bb, r, i: (0, 0, 0)),
        ],
        out_specs=[out_spec, out_spec],
        out_shape=[jax.ShapeDtypeStruct((b, ls, d * gw), F32)] * 2,
        compiler_params=_cparams(("parallel", "parallel", "arbitrary")),
        name=f"attn_d{d}",
    )(zv, zv, zv, zv, zv, bias)
    return o.reshape(b * s, gw), l.reshape(b * s, gw)


def _attn_sample_kernel(q_ref, k_ref, v_ref, c0_ref, c1_ref, c2_ref, brow_ref, bself_ref, a_ref):
    caches = (c0_ref, c1_ref, c2_ref)
    gw = HEADS * HD_A
    for h in range(HEADS):
        outs, lses = [], []
        for g in range(N_GROUPS):
            col = g * gw + h * HD_A
            q = q_ref[:, col:col + HD_A]
            kn = k_ref[:, col:col + HD_A]
            vn = v_ref[:, col:col + HD_A]
            kc = caches[g][:, h * HD_A:(h + 1) * HD_A]
            vc = caches[g][:, gw + h * HD_A:gw + (h + 1) * HD_A]
            sc = jnp.sum(kc * q, axis=-1, keepdims=True) + brow_ref[g, :, h:h + 1]
            s0 = jnp.sum(kn * q, axis=-1, keepdims=True) + bself_ref[g, :, h:h + 1][0:1]
            m = jnp.maximum(jnp.max(sc, axis=0, keepdims=True), s0)
            e = jnp.exp(sc - m)
            e0 = jnp.exp(s0 - m)
            den = jnp.sum(e, axis=0, keepdims=True) + e0
            outs.append((jnp.sum(e * vc, axis=0, keepdims=True) + e0 * vn) / den)
            lses.append(m + jnp.log(den))
        mm = jnp.maximum(jnp.maximum(lses[0], lses[1]), lses[2])
        ws = [jnp.exp(l - mm) for l in lses]
        wsum = ws[0] + ws[1] + ws[2]
        a_ref[:, h * HD_A:(h + 1) * HD_A] = (ws[0] * outs[0] + ws[1] * outs[1] + ws[2] * outs[2]) / wsum


def _attn_sample(qa, ka, va, caches, brow, bself):
    db = qa.shape[0]
    gw = HEADS * HD_A
    cviews, cspecs = [], []
    for c, (window, d) in zip(caches, ATTN_GROUPS):
        assert c.shape[1] == window, "sample attention expects a full window of cached rows"
        cviews.append(c.reshape(db, window // d, d * 2 * gw))
        cspecs.append(pl.BlockSpec((None, ATTN_BLK, 2 * gw), lambda b: (b, 0, 0)))
    tok = pl.BlockSpec((None, 1, QA_W), lambda b: (b, 0, 0))
    return pl.pallas_call(
        _attn_sample_kernel,
        grid=(db,),
        in_specs=[tok, tok, tok] + cspecs + [
            pl.BlockSpec((N_GROUPS, ATTN_BLK, HEADS), lambda b: (0, 0, 0)),
            pl.BlockSpec((N_GROUPS, 8, HEADS), lambda b: (0, 0, 0)),
        ],
        out_specs=pl.BlockSpec((None, 1, gw), lambda b: (b, 0, 0)),
        out_shape=jax.ShapeDtypeStruct((db, 1, gw), F32),
        compiler_params=_cparams(("parallel",)),
        name="attn_sample",
    )(qa, ka, va, *cviews, brow, bself)


def _ret_epilogue(o, gate):
    on = o * lax.rsqrt(jnp.mean(o * o, axis=-1, keepdims=True) + EPS)
    return on * (gate / (1.0 + jnp.exp(-gate)))


def _ret_prompt_kernel(q_ref, k_ref, va_ref, vb_ref, ga_ref, gb_ref, dec_ref, xi_ref, zeta_ref, o_ref, rout_ref, r_scr):
    c = pl.program_id(1)
    heads_per_tile = COL_TILE // DV_R

    @pl.when(c == 0)
    def _():
        r_scr[...] = jnp.zeros_like(r_scr)

    for h in range(HR):
        qs = slice(h * DK_R, (h + 1) * DK_R)
        vs = slice(h * DV_R, (h + 1) * DV_R)
        ts = slice((h % heads_per_tile) * DV_R, (h % heads_per_tile + 1) * DV_R)
        v_ref, g_ref = ((va_ref, ga_ref), (vb_ref, gb_ref))[h // heads_per_tile]
        q = q_ref[:, qs]
        k = k_ref[:, qs]
        qb = q.astype(BF16)
        vb = v_ref[:, ts].astype(BF16)
        r = r_scr[h]
        xi = xi_ref[h]
        s = _dot_nt(qb, k.astype(BF16)) * dec_ref[h]
        o = _dot(s.astype(BF16), vb) + _dot(qb, r.astype(BF16)) * xi
        kz = (k * zeta_ref[h]).T
        r_scr[h] = xi[RET_CHUNK - 1:RET_CHUNK, :] * r + _dot(kz.astype(BF16), vb)
        o_ref[:, vs] = _ret_epilogue(o, g_ref[:, ts])

    @pl.when(c == pl.num_programs(1) - 1)
    def _():
        rout_ref[...] = r_scr[...]


def _ret_prompt(z3, dec, xi, zeta, b, s):
    nc = s // RET_CHUNK
    qk_w = HR * DK_R
    tab = lambda w: pl.BlockSpec((HR, RET_CHUNK, w), lambda bb, c: (0, 0, 0))
    o, r = pl.pallas_call(
        _ret_prompt_kernel,
        grid=(b, nc),
        in_specs=[
            pl.BlockSpec((None, RET_CHUNK, qk_w), lambda bb, c: (bb, c, CT_QR)),
            pl.BlockSpec((None, RET_CHUNK, qk_w), lambda bb, c: (bb, c, CT_KR)),
            pl.BlockSpec((None, RET_CHUNK, COL_TILE), lambda bb, c: (bb, c, CT_VR)),
            pl.BlockSpec((None, RET_CHUNK, COL_TILE), lambda bb, c: (bb, c, CT_VR + 1)),
            pl.BlockSpec((None, RET_CHUNK, COL_TILE), lambda bb, c: (bb, c, CT_GR)),
            pl.BlockSpec((None, RET_CHUNK, COL_TILE), lambda bb, c: (bb, c, CT_GR + 1)),
            tab(RET_CHUNK), tab(DV_R), tab(DK_R),
        ],
        out_specs=[
            pl.BlockSpec((None, RET_CHUNK, RV_W), lambda bb, c: (bb, c, 0)),
            pl.BlockSpec((None, HR, DK_R, DV_R), lambda bb, c: (bb, 0, 0, 0)),
        ],
        out_shape=[jax.ShapeDtypeStruct((b, s, RV_W), F32), jax.ShapeDtypeStruct((b, HR, DK_R, DV_R), F32)],
        scratch_shapes=[pltpu.VMEM((HR, DK_R, DV_R), F32)],
        compiler_params=_cparams(("parallel", "arbitrary")),
        name="ret_prompt",
    )(z3, z3, z3, z3, z3, z3, dec, xi, zeta)
    return o.reshape(b * s, RV_W), r


def _ret_sample_kernel(q_ref, k_ref, v_ref, g_ref, r0_ref, gam_ref, o_ref, rout_ref):
    eye = (lax.broadcasted_iota(jnp.int32, (DK_R, DK_R), 0) == lax.broadcasted_iota(jnp.int32, (DK_R, DK_R), 1))
    for h in range(HR):
        qs = slice(h * DK_R, (h + 1) * DK_R)
        vs = slice(h * DV_R, (h + 1) * DV_R)
        q = q_ref[:, qs]
        k = k_ref[:, qs]
        v = v_ref[:, vs]
        r = r0_ref[h]
        gam = gam_ref[h]
        qcol = jnp.sum(jnp.where(eye, jnp.broadcast_to(q, (DK_R, DK_R)), 0.0), axis=-1, keepdims=True)
        kcol = jnp.sum(jnp.where(eye, jnp.broadcast_to(k, (DK_R, DK_R)), 0.0), axis=-1, keepdims=True)
        qk = jnp.sum(q * k, axis=-1, keepdims=True)
        o = qk * v + jnp.sum(qcol * r, axis=0, keepdims=True) * gam
        rout_ref[h] = gam * r + kcol * v
        o_ref[:, vs] = _ret_epilogue(o, g_ref[:, vs])


def _ret_sample(qr, kr, vr, gr, r0, gam):
    db = qr.shape[0]
    qk_w = HR * DK_R
    return pl.pallas_call(
        _ret_sample_kernel,
        grid=(db,),
        in_specs=[
            pl.BlockSpec((None, 1, qk_w), lambda b: (b, 0, 0)),
            pl.BlockSpec((None, 1, qk_w), lambda b: (b, 0, 0)),
            pl.BlockSpec((None, 1, RV_W), lambda b: (b, 0, 0)),
            pl.BlockSpec((None, 1, RV_W), lambda b: (b, 0, 0)),
            pl.BlockSpec((None, HR, DK_R, DV_R), lambda b: (b, 0, 0, 0)),
            pl.BlockSpec((HR, 1, DV_R), lambda b: (0, 0, 0)),
        ],
        out_specs=[
            pl.BlockSpec((None, 1, RV_W), lambda b: (b, 0, 0)),
            pl.BlockSpec((None, HR, DK_R, DV_R), lambda b: (b, 0, 0, 0)),
        ],
        out_shape=[jax.ShapeDtypeStruct((db, 1, RV_W), F32), jax.ShapeDtypeStruct((db, HR, DK_R, DV_R), F32)],
        compiler_params=_cparams(("parallel",)),
        name="ret_sample",
    )(qr, kr, vr, gr, r0, gam)


def _merge_kernel(*refs, combined):
    if combined:
        a_ref = refs[0]
        rest = refs[1:]
        a = a_ref[...]
    else:
        o0, o1, o2, l0, l1, l2 = refs[:6]
        rest = refs[6:]
        la, lb, lc = l0[...], l1[...], l2[...]
        m = jnp.maximum(jnp.maximum(la, lb), lc)
        ea, eb, ec = jnp.exp(la - m), jnp.exp(lb - m), jnp.exp(lc - m)
        a = (ea * o0[...] + eb * o1[...] + ec * o2[...]) / (ea + eb + ec)
    (ro_ref, ga0, ga1, gb0, gb1, x_ref, woa_ref, wor_ref, wout_ref, wq_ref, bg_ref, n2_ref, x1_ref, h2_ref, pq_ref) = rest
    ba = _dot(a.astype(BF16), woa_ref[...])
    br = _dot(ro_ref[...].astype(BF16), wor_ref[...])
    ga = jnp.concatenate([ga0[...], ga1[...]], axis=-1) + bg_ref[:, :D_MODEL]
    gb = jnp.concatenate([gb0[...], gb1[...]], axis=-1) + bg_ref[:, D_MODEL:]
    mixed = ba / (1.0 + jnp.exp(-ga)) + br / (1.0 + jnp.exp(-gb))
    x1 = x_ref[...] + _dot(mixed.astype(BF16), wout_ref[...])
    x1_ref[...] = x1
    h2 = (x1 * lax.rsqrt(jnp.mean(x1 * x1, axis=-1, keepdims=True) + EPS) * n2_ref[...]).astype(BF16)
    h2_ref[...] = h2
    pq_ref[...] = _dot(h2, wq_ref[...])


def _merge(attn_in, ro, z, x, woa, wor, wout, wq, bg, n2, tm):
    t = x.shape[0]
    gw = HEADS * HD_A
    combined = len(attn_in) == 1
    row = lambda w, c=0: pl.BlockSpec((tm, w), lambda i, c=c: (i, c))
    full = lambda a: pl.BlockSpec(a.shape, lambda i: (0, 0))
    in_specs = [row(gw)] * len(attn_in) + [
        row(RV_W),
        row(COL_TILE, CT_GATE), row(COL_TILE, CT_GATE + 1), row(COL_TILE, CT_GATE + 2), row(COL_TILE, CT_GATE + 3),
        row(D_MODEL),
        full(woa), full(wor), full(wout), full(wq), full(bg), full(n2),
    ]
    return pl.pallas_call(
        functools.partial(_merge_kernel, combined=combined),
        grid=(t // tm,),
        in_specs=in_specs,
        out_specs=[row(D_MODEL), row(D_MODEL), row(wq.shape[1])],
        out_shape=[jax.ShapeDtypeStruct((t, D_MODEL), F32), jax.ShapeDtypeStruct((t, D_MODEL), BF16),
                   jax.ShapeDtypeStruct((t, wq.shape[1]), F32)],
        compiler_params=_cparams(("parallel",)),
        name="merge",
    )(*attn_in, ro, z, z, z, z, x, woa, wor, wout, wq, bg, n2)


def _topk_rank(s):
    key = lax.broadcasted_iota(jnp.int32, s.shape, 0).astype(F32)
    rank = jnp.full(s.shape, float(PEER_TOPK), F32)
    vals = []
    for k in range(PEER_TOPK):
        m = jnp.max(s, axis=0, keepdims=True)
        first = jnp.min(jnp.where(s == m, key, float(N_KEYS)), axis=0, keepdims=True)
        hit = key == first
        rank = jnp.where(hit, float(k), rank)
        s = jnp.where(hit, -jnp.inf, s)
        vals.append(m)
    return rank, vals


def _peer_select_kernel(pq_ref, sk_ref, row_ref, tile_ref):
    ntok = pq_ref.shape[0]
    hi = lax.Precision.HIGHEST
    sub = lax.broadcasted_iota(jnp.int32, (8, ntok), 0).astype(F32)
    flat = jnp.concatenate([sub, 8.0 + sub, 16.0 + sub, 32.0 + sub, 48.0 + sub,
                            (8.0 + sub) * 16.0, sub * 16.0, sub * 16.0 + 1.0, sub * 16.0 + 2.0], axis=0)
    ninf = jnp.full((8, ntok), -jnp.inf, F32)
    for hd in range(PEER_HEADS):
        sc = []
        for p in range(2):
            col = (hd * 2 + p) * N_KEYS
            sc.append(_dot_nt(sk_ref[p], pq_ref[:, col:col + N_KEYS], hi))
        rank1, sv1 = _topk_rank(sc[0])
        rank2, sv2 = _topk_rank(sc[1])
        a_lo, a_hi = jnp.concatenate(sv1[:8], axis=0), jnp.concatenate(sv1[8:], axis=0)
        b_lo, b_hi = jnp.concatenate(sv2[:8], axis=0), jnp.concatenate(sv2[8:], axis=0)
        cand = jnp.concatenate([
            sv1[0] + b_lo, sv1[0] + b_hi, sv1[1] + b_lo,
            jnp.where(sub < 5.0, sv1[2] + b_lo, ninf),
            jnp.where(sub < 4.0, sv1[3] + b_lo, ninf),
            a_hi + sv2[0],
            jnp.where(sub >= 4.0, a_lo + sv2[0], ninf),
            jnp.where(sub >= 4.0, a_lo + sv2[1], ninf),
            jnp.where(sub == 4.0, a_lo + sv2[2], ninf),
        ], axis=0)
        work = cand
        chosen = jnp.zeros(cand.shape, F32)
        for _ in range(PEER_TOPK):
            m = jnp.max(work, axis=0, keepdims=True)
            first = jnp.min(jnp.where(work == m, flat, float(PEER_TOPK * PEER_TOPK)), axis=0, keepdims=True)
            hit = flat == first
            chosen = jnp.where(hit, 1.0, chosen)
            work = jnp.where(hit, -jnp.inf, work)
        top = sv1[0] + sv2[0]
        zsum = jnp.sum(chosen * jnp.exp(cand - top), axis=0, keepdims=True)
        cnt_rows = [jnp.sum(chosen[0:16], axis=0, keepdims=True)]
        cnt_rows += [jnp.sum(chosen[8 * g:8 * g + 8], axis=0, keepdims=True) for g in (2, 3, 4)]
        cnt_mid = chosen[48:56] + chosen[56:64] + chosen[64:72]
        cnt_rows += [cnt_mid[a:a + 1] for a in range(4, 8)]
        cnt_rows += [chosen[40 + a:41 + a] for a in range(8)]
        cnt1 = jnp.zeros((N_KEYS, ntok), F32)
        for a in range(PEER_TOPK):
            cnt1 = jnp.where(rank1 == float(a), cnt_rows[a], cnt1)
        row_ref[hd, 0] = cnt1
        row_ref[hd, 1] = jnp.exp(sc[0] - sv1[0]) / zsum
        tile_ref[hd, 0] = rank2
        tile_ref[hd, 1] = jnp.exp(sc[1] - sv2[0])


def _peer_select(pq, sk):
    t = pq.shape[0]
    tl = PEER_TOK_LANES
    shape = (t // tl, PEER_HEADS, 2, N_KEYS, tl)
    spec = pl.BlockSpec((None, PEER_HEADS, 2, N_KEYS, tl), lambda i: (i, 0, 0, 0, 0))
    return pl.pallas_call(
        _peer_select_kernel,
        grid=(t // tl,),
        in_specs=[
            pl.BlockSpec((tl, pq.shape[1]), lambda i: (i, 0)),
            pl.BlockSpec(sk.shape, lambda i: (0, 0, 0)),
        ],
        out_specs=[spec, spec],
        out_shape=[jax.ShapeDtypeStruct(shape, F32), jax.ShapeDtypeStruct(shape, F32)],
        compiler_params=_cparams(("parallel",)),
        name="peer_select",
    )(pq, sk)


def _gelu(x):
    return 0.5 * x * (1.0 + lax.erf(x * np.float32(2.0 ** -0.5)))


def _peer_build_gated(e, half, row_ref, tile_ref, u_ref, h_ref, act_ref, g_ref, nsub):
    keys_per_blk = PEER_EXPERT_BLK // N_KEYS
    half_rows = PEER_EXPERT_BLK // 2
    chunks = half_rows // PEER_EXPERT_CHUNK
    for c in range(chunks):
        rows = slice(half * half_rows + c * PEER_EXPERT_CHUNK, half * half_rows + (c + 1) * PEER_EXPERT_CHUNK)
        act_ref[rows, :] = _gelu(_dot_nt(u_ref[rows, :], h_ref[...]))
    for kk in range(half_rows // N_KEYS):
        i = e * keys_per_blk + half * (half_rows // N_KEYS) + kk
        for t in range(nsub):
            lanes = slice(t * PEER_TOK_LANES, (t + 1) * PEER_TOK_LANES)
            for js in range(N_KEYS // PEER_KEY_SLICE):
                keys = slice(js * PEER_KEY_SLICE, (js + 1) * PEER_KEY_SLICE)
                w = None
                for hd in range(PEER_HEADS):
                    cnt = row_ref[t, hd, 0, pl.ds(i, 1), :]
                    e1 = row_ref[t, hd, 1, pl.ds(i, 1), :]
                    term = jnp.where(tile_ref[t, hd, 0, keys, :] < cnt, e1 * tile_ref[t, hd, 1, keys, :], 0.0)
                    w = term if w is None else w + term
                r0 = kk * N_KEYS + js * PEER_KEY_SLICE
                a0 = half * half_rows + r0
                g_ref[r0:r0 + PEER_KEY_SLICE, lanes] = (w * act_ref[a0:a0 + PEER_KEY_SLICE, lanes]).astype(BF16)


def _peer_dense_kernel(h_ref, row_ref, tile_ref, u_ref, vt_ref, vtp_ref, x_ref, out_ref,
                       act_ref, ga_ref, gb_ref, acc_ref, *, nsub):
    e = pl.program_id(1)
    last = pl.num_programs(1) - 1
    half_rows = PEER_EXPERT_BLK // 2

    @pl.when(e == 0)
    def _():
        acc_ref[...] = jnp.zeros_like(acc_ref)
        gb_ref[...] = jnp.zeros_like(gb_ref)

    @pl.when(e < last)
    def _():
        acc_ref[...] += _dot(vtp_ref[:, half_rows:], gb_ref[...])
        _peer_build_gated(e, 0, row_ref, tile_ref, u_ref, h_ref, act_ref, ga_ref, nsub)
        acc_ref[...] += _dot(vt_ref[:, :half_rows], ga_ref[...])
        _peer_build_gated(e, 1, row_ref, tile_ref, u_ref, h_ref, act_ref, gb_ref, nsub)

    @pl.when(e == last)
    def _():
        out_ref[...] = x_ref[...] + (acc_ref[...] + _dot(vtp_ref[:, half_rows:], gb_ref[...])).T


def _peer_dense(h2, sel_rows, sel_tiles, u_bf, vt_bf, x1, tb):
    t = h2.shape[0]
    nsub = tb // PEER_TOK_LANES
    eb = PEER_EXPERT_BLK
    nblk = N_EXPERTS // eb
    sel_spec = pl.BlockSpec((nsub, PEER_HEADS, 2, N_KEYS, PEER_TOK_LANES), lambda i, e: (i, 0, 0, 0, 0))
    return pl.pallas_call(
        functools.partial(_peer_dense_kernel, nsub=nsub),
        grid=(t // tb, nblk + 1),
        in_specs=[
            pl.BlockSpec((tb, D_MODEL), lambda i, e: (i, 0)),
            sel_spec,
            sel_spec,
            pl.BlockSpec((eb, D_MODEL), lambda i, e: (jnp.minimum(e, nblk - 1), 0)),
            pl.BlockSpec((D_MODEL, eb), lambda i, e: (0, jnp.minimum(e, nblk - 1))),
            pl.BlockSpec((D_MODEL, eb), lambda i, e: (0, jnp.maximum(e - 1, 0))),
            pl.BlockSpec((tb, D_MODEL), lambda i, e: (i, 0)),
        ],
        out_specs=pl.BlockSpec((tb, D_MODEL), lambda i, e: (i, 0)),
        out_shape=jax.ShapeDtypeStruct((t, D_MODEL), F32),
        scratch_shapes=[
            pltpu.VMEM((eb, tb), F32),
            pltpu.VMEM((eb // 2, tb), BF16),
            pltpu.VMEM((eb // 2, tb), BF16),
            pltpu.VMEM((D_MODEL, tb), F32),
        ],
        compiler_params=_cparams(("parallel", "arbitrary")),
        name="peer_dense",
    )(h2, sel_rows, sel_tiles, u_bf, vt_bf, vt_bf, x1)


def _peer(h2, pq, x1, sk, u_bf, vt_bf):
    t = h2.shape[0]
    tb = min(512, -(-t // PEER_TOK_LANES) * PEER_TOK_LANES)
    t_pad = -(-t // tb) * tb
    if t_pad != t:
        h2 = jnp.pad(h2, ((0, t_pad - t), (0, 0)))
        x1 = jnp.pad(x1, ((0, t_pad - t), (0, 0)))
        pq = jnp.pad(pq, ((0, t_pad - t), (0, 0)))
    sel_rows, sel_tiles = _peer_select(pq, sk)
    return _peer_dense(h2, sel_rows, sel_tiles, u_bf, vt_bf, x1, tb)[:t]


def _t5_bucket(dist):
    dist = np.asarray(dist, np.int64)
    max_exact = N_BUCKETS // 2
    ratio = np.log(np.maximum(dist, 1) / max_exact) / np.log(REL_MAX_DIST / max_exact)
    large = np.minimum(max_exact + (ratio * (N_BUCKETS - max_exact)).astype(np.int64), N_BUCKETS - 1)
    return np.where(dist < max_exact, dist, large).astype(np.int32)


def _rotary_tables(pos):
    half = DK_R // 2
    ang = jnp.asarray(1.0 / (10000.0 ** np.linspace(0.0, 1.0, half)), F32)
    theta = pos.astype(F32)[:, None] * ang[None, :]
    c, s = jnp.cos(theta), jnp.sin(theta)
    return jnp.concatenate([c, c], axis=-1), jnp.concatenate([-s, s], axis=-1)


def _retention_tables(length, log_g):
    n = np.arange(length)
    diff = n[:, None] - n[None, :]
    decay = jnp.where(jnp.asarray(diff >= 0)[None],
                      jnp.exp(jnp.asarray(np.maximum(diff, 0), F32)[None] * log_g[:, None, None]), 0.0)
    xi = jnp.exp(jnp.asarray(n + 1, F32)[None, :] * log_g[:, None])
    zeta = jnp.exp(jnp.asarray(n[::-1].copy(), F32)[None, :] * log_g[:, None])
    xi = jnp.broadcast_to(xi[:, :, None], (HR, length, DV_R))
    zeta = jnp.broadcast_to(zeta[:, :, None], (HR, length, DK_R))
    return decay, xi, zeta


def _prompt_bias(bias_g, g, d):
    steps = ATTN_GROUPS[g][0] // d
    qi = np.arange(ATTN_BLK)[:, None]
    ki = np.arange(2 * ATTN_BLK)[None, :]
    m = ATTN_BLK + qi - ki
    return jnp.transpose(bias_g[:, g][_t5_bucket(np.clip(m, 0, steps) * d)], (2, 0, 1))


def _sample_bias(bias_g):
    rows, selfs = [], []
    for g, (window, d) in enumerate(ATTN_GROUPS):
        steps = window // d
        assert steps == ATTN_BLK
        j = steps - np.arange(steps)
        rows.append(bias_g[:, g][_t5_bucket(j * d)])
        selfs.append(jnp.broadcast_to(bias_g[:, g][_t5_bucket(np.zeros(1))], (8, HEADS)))
    return jnp.stack(rows), jnp.stack(selfs)


def kernel(x_prompt, x_sample, cache_kv_d1, cache_kv_d4, cache_kv_d16, state_ret, rel_bias, norm1_w, w_in, b_gate, q_norm_w, k_norm_w, w_o_attn, w_o_ret, w_out, norm2_w, peer_w_q, peer_sub_keys, peer_u, peer_v):
    b, s, _ = x_prompt.shape
    db, ds, _ = x_sample.shape
    assert ds == 1, "the sample group is one new token per sequence"
    depth = w_in.shape[0]
    gw = HEADS * HD_A
    caches = (cache_kv_d1, cache_kv_d4, cache_kv_d16)

    bias_g = rel_bias.reshape(N_BUCKETS, N_GROUPS, HEADS).astype(F32)
    log_g = jnp.asarray(np.log(1.0 - 2.0 ** (-5.0 - np.arange(HR))), F32)
    cos_p, sin_p = _rotary_tables(jnp.arange(s, dtype=jnp.int32))
    cos_s, sin_s = _rotary_tables(jnp.full((db,), PAST_LEN, jnp.int32))
    dec_p, xi_p, zeta_p = _retention_tables(RET_CHUNK, log_g)
    gam_s = _retention_tables(1, log_g)[1]
    bias_p = [_prompt_bias(bias_g, g, d) for g, (_, d) in enumerate(ATTN_GROUPS)]
    brow_s, bself_s = _sample_bias(bias_g)

    tm_p = 512 if (b * s) % 512 == 0 else ATTN_BLK
    xp = x_prompt.reshape(b * s, D_MODEL)
    xs = x_sample.reshape(db, D_MODEL)
    kv_p = [[] for _ in range(N_GROUPS)]
    kv_s = [[] for _ in range(N_GROUPS)]
    ret_p, ret_s = [], []

    for l in range(depth):
        w_in_bf = w_in[l].astype(BF16)
        ones = jnp.ones((COL_TILE,), F32)
        colw = jnp.stack(
            [jnp.tile(q_norm_w[l, g], HEADS) * (HD_A ** -0.5) for g in range(N_GROUPS)]
            + [jnp.tile(k_norm_w[l, g], HEADS) for g in range(N_GROUPS)]
            + [ones] * (CT_KR - CT_V) + [ones * (DK_R ** -0.5)] + [ones] * (N_COL_TILES - CT_KR - 1)
        ).reshape(N_COL_TILES, 1, COL_TILE)
        n1 = norm1_w[l].reshape(1, D_MODEL)
        n2 = norm2_w[l].reshape(1, D_MODEL)
        bg = b_gate[l].reshape(1, 2 * D_MODEL)
        woa, wor, wout = w_o_attn[l].astype(BF16), w_o_ret[l].astype(BF16), w_out[l].astype(BF16)
        wq = peer_w_q[l].astype(BF16)
        sk = peer_sub_keys[l]
        u_bf = peer_u[l].astype(BF16)
        vt_bf = peer_v[l].T.astype(BF16)

        zp = _inproj(xp, n1, w_in_bf, colw, cos_p, sin_p, tm_p, s // tm_p)
        zp3 = zp.reshape(b, s, IN_COLS)
        attn_in = [None] * (2 * N_GROUPS)
        for g, (window, d) in enumerate(ATTN_GROUPS):
            attn_in[g], attn_in[N_GROUPS + g] = _attn_prompt(zp3, g, d, bias_p[g], b, s)
            keep = min(window, s)
            k_rows = zp3[:, s - keep:, QA_W + g * gw:QA_W + (g + 1) * gw].reshape(b, keep, HEADS, HD_A)
            v_rows = zp3[:, s - keep:, 2 * QA_W + g * gw:2 * QA_W + (g + 1) * gw].reshape(b, keep, HEADS, HD_A)
            kv_p[g].append(jnp.stack([k_rows, v_rows], axis=2))
        ro, r_new = _ret_prompt(zp3, dec_p, xi_p, zeta_p, b, s)
        ret_p.append(r_new)
        x1, h2, pq = _merge(attn_in, ro, zp, xp, woa, wor, wout, wq, bg, n2, 256 if (b * s) % 256 == 0 else ATTN_BLK)
        xp = _peer(h2, pq, x1, sk, u_bf, vt_bf)

        zs = _inproj(xs, n1, w_in_bf, colw, cos_s, sin_s, db, 1)
        zs3 = zs.reshape(db, 1, IN_COLS)
        qa_s, ka_s, va_s = zs3[:, :, :QA_W], zs3[:, :, QA_W:2 * QA_W], zs3[:, :, 2 * QA_W:3 * QA_W]
        c0 = 3 * QA_W
        qr_s, kr_s = zs3[:, :, c0:c0 + RQ_W], zs3[:, :, c0 + RQ_W:c0 + 2 * RQ_W]
        c1 = c0 + 2 * RQ_W
        vr_s, gr_s = zs3[:, :, c1:c1 + RV_W], zs3[:, :, c1 + RV_W:c1 + 2 * RV_W]
        a_s = _attn_sample(qa_s, ka_s, va_s, [c[l] for c in caches], brow_s, bself_s)
        for g in range(N_GROUPS):
            k_rows = ka_s[:, :, g * gw:(g + 1) * gw].reshape(db, 1, HEADS, HD_A)
            v_rows = va_s[:, :, g * gw:(g + 1) * gw].reshape(db, 1, HEADS, HD_A)
            kv_s[g].append(jnp.stack([k_rows, v_rows], axis=2))
        ro_s, r_s = _ret_sample(qr_s, kr_s, vr_s, gr_s, state_ret[l], gam_s)
        ret_s.append(r_s)
        x1s, h2s, pqs = _merge([a_s.reshape(db, gw)], ro_s.reshape(db, RV_W), zs, xs, woa, wor, wout, wq, bg, n2, db)
        xs = _peer(h2s, pqs, x1s, sk, u_bf, vt_bf)

    return (xp.reshape(b, s, D_MODEL), xs.reshape(db, 1, D_MODEL),
            jnp.stack(kv_p[0]), jnp.stack(kv_p[1]), jnp.stack(kv_p[2]), jnp.stack(ret_p),
            jnp.stack(kv_s[0]), jnp.stack(kv_s[1]), jnp.stack(kv_s[2]), jnp.stack(ret_s))
```

```python
import functools

import numpy as np
import jax
import jax.numpy as jnp
from jax import lax
from jax.experimental import pallas as pl
from jax.experimental.pallas import tpu as pltpu

F32 = jnp.float32
BF16 = jnp.bfloat16

D_MODEL = 1024
PAST_LEN = 16384
ATTN_GROUPS = ((128, 1), (512, 4), (2048, 16))
N_GROUPS = 3
HEADS = 4
HD_A = 128
ATTN_BLK = 128
ATTN_ROWS = 2048
N_BUCKETS = 32
REL_MAX_DIST = 2048
HR = 4
DK_R = 128
DV_R = 256
RET_CHUNK = 128
N_KEYS = 128
N_EXPERTS = N_KEYS * N_KEYS
PEER_HEADS = 8
PEER_TOPK = 16
EPS = 1e-6
NEG = -1e30

QA_W = N_GROUPS * HEADS * HD_A
RQ_W = HR * DK_R
RV_W = HR * DV_R
IN_COLS = 3 * QA_W + 2 * RQ_W + 2 * RV_W + 2 * D_MODEL
COL_TILE = 512
N_COL_TILES = IN_COLS // COL_TILE
CT_K = QA_W // COL_TILE
CT_V = 2 * QA_W // COL_TILE
CT_QR = 3 * QA_W // COL_TILE
CT_KR = CT_QR + 1
CT_VR = CT_KR + 1
CT_GR = CT_VR + 2
CT_GATE = CT_GR + 2

LANES = 128
SUBLANES = 8
VMEM_LIMIT = 48 * 1024 * 1024

PEER_TOK_LANES = 128
PEER_EXPERT_BLK = 1024


def _dot(a, b, precision=None):
    return jnp.dot(a, b, preferred_element_type=F32, precision=precision)


def _dot_nt(a, b, precision=None):
    return lax.dot_general(a, b, (((1,), (1,)), ((), ())), preferred_element_type=F32, precision=precision)


def _cparams(sem):
    return pltpu.CompilerParams(dimension_semantics=sem, vmem_limit_bytes=VMEM_LIMIT)


def _inproj_kernel(x_ref, n1_ref, w_ref, colw_ref, cos_ref, sin_ref, z_ref, hn_ref):
    j = pl.program_id(1)

    @pl.when(j == 0)
    def _():
        x = x_ref[...]
        ms = jnp.mean(x * x, axis=-1, keepdims=True)
        hn_ref[...] = (x * lax.rsqrt(ms + EPS) * n1_ref[...]).astype(BF16)

    y = _dot(hn_ref[...], w_ref[...])
    cw = colw_ref[...]

    @pl.when(j < CT_V)
    def _():
        for h in range(HEADS):
            sl = slice(h * HD_A, (h + 1) * HD_A)
            yh = y[:, sl]
            ms = jnp.mean(yh * yh, axis=-1, keepdims=True)
            z_ref[:, sl] = yh * lax.rsqrt(ms + EPS) * cw[:, sl]

    @pl.when((j == CT_QR) | (j == CT_KR))
    def _():
        c = cos_ref[...]
        s = sin_ref[...]
        for h in range(HR):
            sl = slice(h * DK_R, (h + 1) * DK_R)
            yh = y[:, sl]
            z_ref[:, sl] = (yh * c + pltpu.roll(yh, DK_R // 2, 1) * s) * cw[:, sl]

    @pl.when(((j >= CT_V) & (j < CT_QR)) | (j >= CT_VR))
    def _():
        z_ref[...] = y


def _inproj(x, n1, w_bf, colw, cos_t, sin_t, tm, pos_blocks):
    t = x.shape[0]
    return pl.pallas_call(
        _inproj_kernel,
        grid=(t // tm, N_COL_TILES),
        in_specs=[
            pl.BlockSpec((tm, D_MODEL), lambda i, j: (i, 0)),
            pl.BlockSpec((1, D_MODEL), lambda i, j: (0, 0)),
            pl.BlockSpec((D_MODEL, COL_TILE), lambda i, j: (0, j)),
            pl.BlockSpec((None, 1, COL_TILE), lambda i, j: (j, 0, 0)),
            pl.BlockSpec((tm, DK_R), lambda i, j: (i % pos_blocks, 0)),
            pl.BlockSpec((tm, DK_R), lambda i, j: (i % pos_blocks, 0)),
        ],
        out_specs=pl.BlockSpec((tm, COL_TILE), lambda i, j: (i, j)),
        out_shape=jax.ShapeDtypeStruct((t, IN_COLS), F32),
        scratch_shapes=[pltpu.VMEM((tm, D_MODEL), BF16)],
        compiler_params=_cparams(("parallel", "arbitrary")),
        name="inproj",
    )(x, n1, w_bf, colw, cos_t, sin_t)


def _attn_tile(q, kp, kc, vp, vc, bias_p, bias_c, prev_ok):
    qi = lax.broadcasted_iota(jnp.int32, (ATTN_BLK, ATTN_BLK), 0)
    ki = lax.broadcasted_iota(jnp.int32, (ATTN_BLK, ATTN_BLK), 1)
    qb = q.astype(BF16)
    sp = _dot_nt(qb, kp.astype(BF16)) + bias_p
    sc = _dot_nt(qb, kc.astype(BF16)) + bias_c
    sp = jnp.where(jnp.logical_and(ki >= qi, prev_ok), sp, NEG)
    sc = jnp.where(ki <= qi, sc, NEG)
    m = jnp.maximum(jnp.max(sp, axis=-1, keepdims=True), jnp.max(sc, axis=-1, keepdims=True))
    pp = jnp.exp(sp - m)
    pc = jnp.exp(sc - m)
    den = jnp.sum(pp, axis=-1, keepdims=True) + jnp.sum(pc, axis=-1, keepdims=True)
    acc = _dot(pp.astype(BF16), vp.astype(BF16)) + _dot(pc.astype(BF16), vc.astype(BF16))
    return acc / den, m + jnp.log(den)


def _attn_prompt_kernel(*refs):
    q_refs, kc_refs, vc_refs, kp_refs, vp_refs = (refs[3 * n:3 * n + 3] for n in range(5))
    bias_ref, a_ref, o_scr, l_scr = refs[15:]
    i = pl.program_id(2)
    for g, (_, d) in enumerate(ATTN_GROUPS):
        span = ATTN_BLK * d
        nblk = ATTN_ROWS // span
        q_ref, kc_ref, vc_ref, kp_ref, vp_ref = q_refs[g], kc_refs[g], vc_refs[g], kp_refs[g], vp_refs[g]

        def tile(idx, carry, g=g, d=d, span=span, nblk=nblk,
                 q_ref=q_ref, kc_ref=kc_ref, vc_ref=vc_ref, kp_ref=kp_ref, vp_ref=vp_ref):
            r = idx % d
            c = idx // d
            rows = pl.ds(c * span + r, ATTN_BLK, stride=d)
            edge = pl.ds(r, ATTN_BLK, stride=d)
            if nblk == 1:
                kp, vp = kp_ref[edge, :], vp_ref[edge, :]
            else:
                inner = pl.ds(jnp.maximum(c - 1, 0) * span + r, ATTN_BLK, stride=d)
                kp = jnp.where(c == 0, kp_ref[edge, :], kc_ref[inner, :])
                vp = jnp.where(c == 0, vp_ref[edge, :], vc_ref[inner, :])
            prev_ok = jnp.logical_or(c > 0, i > 0)
            o, lse = _attn_tile(q_ref[rows, :], kp, kc_ref[rows, :], vp, vc_ref[rows, :],
                                bias_ref[g, :, :ATTN_BLK], bias_ref[g, :, ATTN_BLK:], prev_ok)
            o_scr[g, rows, :] = o
            l_scr[g, rows, :] = jnp.broadcast_to(lse, (ATTN_BLK, HD_A))
            return carry

        lax.fori_loop(0, d * nblk, tile, 0)

    def combine(c, carry):
        rows = pl.ds(pl.multiple_of(c * ATTN_BLK, ATTN_BLK), ATTN_BLK)
        la, lb, lc = l_scr[0, rows, :], l_scr[1, rows, :], l_scr[2, rows, :]
        m = jnp.maximum(jnp.maximum(la, lb), lc)
        ea, eb, ec = jnp.exp(la - m), jnp.exp(lb - m), jnp.exp(lc - m)
        a_ref[rows, :] = (ea * o_scr[0, rows, :] + eb * o_scr[1, rows, :] + ec * o_scr[2, rows, :]) / (ea + eb + ec)
        return carry

    lax.fori_loop(0, ATTN_ROWS // ATTN_BLK, combine, 0)


def _attn_prompt(z3, bias, b, s):
    col = lambda base, g: (lambda bb, h, i, c0=(base + g * HEADS * HD_A) // HD_A: (bb, i, c0 + h))

    def prev_spec(base, g, d):
        span = ATTN_BLK * d
        nblk = ATTN_ROWS // span
        c0 = (base + g * HEADS * HD_A) // HD_A
        return pl.BlockSpec((None, span, HD_A), lambda bb, h, i: (bb, jnp.maximum(i * nblk - 1, 0), c0 + h))

    cur = lambda base, g: pl.BlockSpec((None, ATTN_ROWS, HD_A), col(base, g))
    groups = range(N_GROUPS)
    in_specs = ([cur(0, g) for g in groups] + [cur(QA_W, g) for g in groups] + [cur(2 * QA_W, g) for g in groups]
                + [prev_spec(QA_W, g, d) for g, (_, d) in enumerate(ATTN_GROUPS)]
                + [prev_spec(2 * QA_W, g, d) for g, (_, d) in enumerate(ATTN_GROUPS)]
                + [pl.BlockSpec((N_GROUPS, None, ATTN_BLK, 2 * ATTN_BLK), lambda bb, h, i: (0, h, 0, 0))])
    a = pl.pallas_call(
        _attn_prompt_kernel,
        grid=(b, HEADS, s // ATTN_ROWS),
        in_specs=in_specs,
        out_specs=pl.BlockSpec((None, ATTN_ROWS, HD_A), lambda bb, h, i: (bb, i, h)),
        out_shape=jax.ShapeDtypeStruct((b, s, HEADS * HD_A), F32),
        scratch_shapes=[pltpu.VMEM((N_GROUPS, ATTN_ROWS, HD_A), F32), pltpu.VMEM((N_GROUPS, ATTN_ROWS, HD_A), F32)],
        compiler_params=_cparams(("parallel", "parallel", "arbitrary")),
        name="attn_prompt",
    )(*([z3] * 15), bias)
    return a.reshape(b * s, HEADS * HD_A)


def _attn_sample_kernel(q_ref, k_ref, v_ref, c0_ref, c1_ref, c2_ref, brow_ref, bself_ref, a_ref):
    caches = (c0_ref, c1_ref, c2_ref)
    gw = HEADS * HD_A
    for h in range(HEADS):
        outs, lses = [], []
        for g in range(N_GROUPS):
            col = g * gw + h * HD_A
            q = q_ref[:, col:col + HD_A]
            kn = k_ref[:, col:col + HD_A]
            vn = v_ref[:, col:col + HD_A]
            kc = caches[g][:, h * HD_A:(h + 1) * HD_A]
            vc = caches[g][:, gw + h * HD_A:gw + (h + 1) * HD_A]
            sc = jnp.sum(kc * q, axis=-1, keepdims=True) + brow_ref[g, :, h:h + 1]
            s0 = jnp.sum(kn * q, axis=-1, keepdims=True) + bself_ref[g, :, h:h + 1][0:1]
            m = jnp.maximum(jnp.max(sc, axis=0, keepdims=True), s0)
            e = jnp.exp(sc - m)
            e0 = jnp.exp(s0 - m)
            den = jnp.sum(e, axis=0, keepdims=True) + e0
            outs.append((jnp.sum(e * vc, axis=0, keepdims=True) + e0 * vn) / den)
            lses.append(m + jnp.log(den))
        mm = jnp.maximum(jnp.maximum(lses[0], lses[1]), lses[2])
        ws = [jnp.exp(l - mm) for l in lses]
        wsum = ws[0] + ws[1] + ws[2]
        a_ref[:, h * HD_A:(h + 1) * HD_A] = (ws[0] * outs[0] + ws[1] * outs[1] + ws[2] * outs[2]) / wsum


def _attn_sample(qa, ka, va, caches, brow, bself):
    db = qa.shape[0]
    gw = HEADS * HD_A
    cviews, cspecs = [], []
    for c, (window, d) in zip(caches, ATTN_GROUPS):
        assert c.shape[1] == window, "sample attention expects a full window of cached rows"
        cviews.append(c.reshape(db, window // d, d * 2 * gw))
        cspecs.append(pl.BlockSpec((None, ATTN_BLK, 2 * gw), lambda b: (b, 0, 0)))
    tok = pl.BlockSpec((None, 1, QA_W), lambda b: (b, 0, 0))
    return pl.pallas_call(
        _attn_sample_kernel,
        grid=(db,),
        in_specs=[tok, tok, tok] + cspecs + [
            pl.BlockSpec((N_GROUPS, ATTN_BLK, HEADS), lambda b: (0, 0, 0)),
            pl.BlockSpec((N_GROUPS, 8, HEADS), lambda b: (0, 0, 0)),
        ],
        out_specs=pl.BlockSpec((None, 1, gw), lambda b: (b, 0, 0)),
        out_shape=jax.ShapeDtypeStruct((db, 1, gw), F32),
        compiler_params=_cparams(("parallel",)),
        name="attn_sample",
    )(qa, ka, va, *cviews, brow, bself)


def _ret_epilogue(o, gate):
    on = o * lax.rsqrt(jnp.mean(o * o, axis=-1, keepdims=True) + EPS)
    return on * (gate / (1.0 + jnp.exp(-gate)))


def _ret_prompt_kernel(q_ref, k_ref, va_ref, vb_ref, ga_ref, gb_ref, dec_ref, xi_ref, zeta_ref, o_ref, rout_ref, r_scr):
    c = pl.program_id(1)
    heads_per_tile = COL_TILE // DV_R

    @pl.when(c == 0)
    def _():
        r_scr[...] = jnp.zeros_like(r_scr)

    for h in range(HR):
        qs = slice(h * DK_R, (h + 1) * DK_R)
        vs = slice(h * DV_R, (h + 1) * DV_R)
        ts = slice((h % heads_per_tile) * DV_R, (h % heads_per_tile + 1) * DV_R)
        v_ref, g_ref = ((va_ref, ga_ref), (vb_ref, gb_ref))[h // heads_per_tile]
        q = q_ref[:, qs]
        k = k_ref[:, qs]
        qb = q.astype(BF16)
        vb = v_ref[:, ts].astype(BF16)
        r = r_scr[h]
        xi = xi_ref[h]
        s = _dot_nt(qb, k.astype(BF16)) * dec_ref[h]
        o = _dot(s.astype(BF16), vb) + _dot(qb, r.astype(BF16)) * xi
        kz = (k * zeta_ref[h]).T
        r_scr[h] = xi[RET_CHUNK - 1:RET_CHUNK, :] * r + _dot(kz.astype(BF16), vb)
        o_ref[:, vs] = _ret_epilogue(o, g_ref[:, ts])

    @pl.when(c == pl.num_programs(1) - 1)
    def _():
        rout_ref[...] = r_scr[...]


def _ret_prompt(z3, dec, xi, zeta, b, s):
    nc = s // RET_CHUNK
    qk_w = HR * DK_R
    tab = lambda w: pl.BlockSpec((HR, RET_CHUNK, w), lambda bb, c: (0, 0, 0))
    o, r = pl.pallas_call(
        _ret_prompt_kernel,
        grid=(b, nc),
        in_specs=[
            pl.BlockSpec((None, RET_CHUNK, qk_w), lambda bb, c: (bb, c, CT_QR)),
            pl.BlockSpec((None, RET_CHUNK, qk_w), lambda bb, c: (bb, c, CT_KR)),
            pl.BlockSpec((None, RET_CHUNK, COL_TILE), lambda bb, c: (bb, c, CT_VR)),
            pl.BlockSpec((None, RET_CHUNK, COL_TILE), lambda bb, c: (bb, c, CT_VR + 1)),
            pl.BlockSpec((None, RET_CHUNK, COL_TILE), lambda bb, c: (bb, c, CT_GR)),
            pl.BlockSpec((None, RET_CHUNK, COL_TILE), lambda bb, c: (bb, c, CT_GR + 1)),
            tab(RET_CHUNK), tab(DV_R), tab(DK_R),
        ],
        out_specs=[
            pl.BlockSpec((None, RET_CHUNK, RV_W), lambda bb, c: (bb, c, 0)),
            pl.BlockSpec((None, HR, DK_R, DV_R), lambda bb, c: (bb, 0, 0, 0)),
        ],
        out_shape=[jax.ShapeDtypeStruct((b, s, RV_W), F32), jax.ShapeDtypeStruct((b, HR, DK_R, DV_R), F32)],
        scratch_shapes=[pltpu.VMEM((HR, DK_R, DV_R), F32)],
        compiler_params=_cparams(("parallel", "arbitrary")),
        name="ret_prompt",
    )(z3, z3, z3, z3, z3, z3, dec, xi, zeta)
    return o.reshape(b * s, RV_W), r


def _ret_sample_kernel(q_ref, k_ref, v_ref, g_ref, r0_ref, gam_ref, o_ref, rout_ref):
    eye = (lax.broadcasted_iota(jnp.int32, (DK_R, DK_R), 0) == lax.broadcasted_iota(jnp.int32, (DK_R, DK_R), 1))
    for h in range(HR):
        qs = slice(h * DK_R, (h + 1) * DK_R)
        vs = slice(h * DV_R, (h + 1) * DV_R)
        q = q_ref[:, qs]
        k = k_ref[:, qs]
        v = v_ref[:, vs]
        r = r0_ref[h]
        gam = gam_ref[h]
        qcol = jnp.sum(jnp.where(eye, jnp.broadcast_to(q, (DK_R, DK_R)), 0.0), axis=-1, keepdims=True)
        kcol = jnp.sum(jnp.where(eye, jnp.broadcast_to(k, (DK_R, DK_R)), 0.0), axis=-1, keepdims=True)
        qk = jnp.sum(q * k, axis=-1, keepdims=True)
        o = qk * v + jnp.sum(qcol * r, axis=0, keepdims=True) * gam
        rout_ref[h] = gam * r + kcol * v
        o_ref[:, vs] = _ret_epilogue(o, g_ref[:, vs])


def _ret_sample(qr, kr, vr, gr, r0, gam):
    db = qr.shape[0]
    qk_w = HR * DK_R
    return pl.pallas_call(
        _ret_sample_kernel,
        grid=(db,),
        in_specs=[
            pl.BlockSpec((None, 1, qk_w), lambda b: (b, 0, 0)),
            pl.BlockSpec((None, 1, qk_w), lambda b: (b, 0, 0)),
            pl.BlockSpec((None, 1, RV_W), lambda b: (b, 0, 0)),
            pl.BlockSpec((None, 1, RV_W), lambda b: (b, 0, 0)),
            pl.BlockSpec((None, HR, DK_R, DV_R), lambda b: (b, 0, 0, 0)),
            pl.BlockSpec((HR, 1, DV_R), lambda b: (0, 0, 0)),
        ],
        out_specs=[
            pl.BlockSpec((None, 1, RV_W), lambda b: (b, 0, 0)),
            pl.BlockSpec((None, HR, DK_R, DV_R), lambda b: (b, 0, 0, 0)),
        ],
        out_shape=[jax.ShapeDtypeStruct((db, 1, RV_W), F32), jax.ShapeDtypeStruct((db, HR, DK_R, DV_R), F32)],
        compiler_params=_cparams(("parallel",)),
        name="ret_sample",
    )(qr, kr, vr, gr, r0, gam)


def _merge_kernel(a_ref, ro_ref, ga0, ga1, gb0, gb1, x_ref, woa_ref, wor_ref, wout_ref, wq_ref, bg_ref, n2_ref,
                  x1_ref, h2_ref, pq_ref):
    a = a_ref[...]
    ba = _dot(a.astype(BF16), woa_ref[...])
    br = _dot(ro_ref[...].astype(BF16), wor_ref[...])
    ga = jnp.concatenate([ga0[...], ga1[...]], axis=-1) + bg_ref[:, :D_MODEL]
    gb = jnp.concatenate([gb0[...], gb1[...]], axis=-1) + bg_ref[:, D_MODEL:]
    mixed = ba / (1.0 + jnp.exp(-ga)) + br / (1.0 + jnp.exp(-gb))
    x1 = x_ref[...] + _dot(mixed.astype(BF16), wout_ref[...])
    x1_ref[...] = x1
    h2 = (x1 * lax.rsqrt(jnp.mean(x1 * x1, axis=-1, keepdims=True) + EPS) * n2_ref[...]).astype(BF16)
    h2_ref[...] = h2
    pq_ref[...] = _dot(h2, wq_ref[...])


def _merge(a, ro, z, x, woa, wor, wout, wq, bg, n2, tm):
    t = x.shape[0]
    gw = HEADS * HD_A
    row = lambda w, c=0: pl.BlockSpec((tm, w), lambda i, c=c: (i, c))
    full = lambda a: pl.BlockSpec(a.shape, lambda i: (0, 0))
    in_specs = [
        row(gw), row(RV_W),
        row(COL_TILE, CT_GATE), row(COL_TILE, CT_GATE + 1), row(COL_TILE, CT_GATE + 2), row(COL_TILE, CT_GATE + 3),
        row(D_MODEL),
        full(woa), full(wor), full(wout), full(wq), full(bg), full(n2),
    ]
    return pl.pallas_call(
        _merge_kernel,
        grid=(t // tm,),
        in_specs=in_specs,
        out_specs=[row(D_MODEL), row(D_MODEL), row(wq.shape[1])],
        out_shape=[jax.ShapeDtypeStruct((t, D_MODEL), F32), jax.ShapeDtypeStruct((t, D_MODEL), BF16),
                   jax.ShapeDtypeStruct((t, wq.shape[1]), F32)],
        compiler_params=_cparams(("parallel",)),
        name="merge",
    )(a, ro, z, z, z, z, x, woa, wor, wout, wq, bg, n2)


def _topk_rank(s):
    key = lax.broadcasted_iota(jnp.int32, s.shape, 0).astype(F32)
    rank = jnp.full(s.shape, float(PEER_TOPK), F32)
    vals = []
    for k in range(PEER_TOPK):
        m = jnp.max(s, axis=0, keepdims=True)
        first = jnp.min(jnp.where(s == m, key, float(N_KEYS)), axis=0, keepdims=True)
        hit = key == first
        rank = jnp.where(hit, float(k), rank)
        s = jnp.where(hit, -jnp.inf, s)
        vals.append(m)
    return rank, vals


def _peer_select_kernel(pq_ref, sk_ref, row_ref, tile_ref):
    ntok = pq_ref.shape[0]
    hi = lax.Precision.HIGHEST
    sub = lax.broadcasted_iota(jnp.int32, (8, ntok), 0).astype(F32)
    flat = jnp.concatenate([sub, 8.0 + sub, 16.0 + sub, 32.0 + sub, 48.0 + sub,
                            (8.0 + sub) * 16.0, sub * 16.0, sub * 16.0 + 1.0, sub * 16.0 + 2.0], axis=0)
    ninf = jnp.full((8, ntok), -jnp.inf, F32)
    for hd in range(PEER_HEADS):
        sc = []
        for p in range(2):
            col = (hd * 2 + p) * N_KEYS
            sc.append(_dot_nt(sk_ref[p], pq_ref[:, col:col + N_KEYS], hi))
        rank1, sv1 = _topk_rank(sc[0])
        rank2, sv2 = _topk_rank(sc[1])
        a_lo, a_hi = jnp.concatenate(sv1[:8], axis=0), jnp.concatenate(sv1[8:], axis=0)
        b_lo, b_hi = jnp.concatenate(sv2[:8], axis=0), jnp.concatenate(sv2[8:], axis=0)
        cand = jnp.concatenate([
            sv1[0] + b_lo, sv1[0] + b_hi, sv1[1] + b_lo,
            jnp.where(sub < 5.0, sv1[2] + b_lo, ninf),
            jnp.where(sub < 4.0, sv1[3] + b_lo, ninf),
            a_hi + sv2[0],
            jnp.where(sub >= 4.0, a_lo + sv2[0], ninf),
            jnp.where(sub >= 4.0, a_lo + sv2[1], ninf),
            jnp.where(sub == 4.0, a_lo + sv2[2], ninf),
        ], axis=0)
        work = cand
        chosen = jnp.zeros(cand.shape, F32)
        for _ in range(PEER_TOPK):
            m = jnp.max(work, axis=0, keepdims=True)
            first = jnp.min(jnp.where(work == m, flat, float(PEER_TOPK * PEER_TOPK)), axis=0, keepdims=True)
            hit = flat == first
            chosen = jnp.where(hit, 1.0, chosen)
            work = jnp.where(hit, -jnp.inf, work)
        top = sv1[0] + sv2[0]
        zsum = jnp.sum(chosen * jnp.exp(cand - top), axis=0, keepdims=True)
        cnt_rows = [jnp.sum(chosen[0:16], axis=0, keepdims=True)]
        cnt_rows += [jnp.sum(chosen[8 * g:8 * g + 8], axis=0, keepdims=True) for g in (2, 3, 4)]
        cnt_mid = chosen[48:56] + chosen[56:64] + chosen[64:72]
        cnt_rows += [cnt_mid[a:a + 1] for a in range(4, 8)]
        cnt_rows += [chosen[40 + a:41 + a] for a in range(8)]
        cnt1 = jnp.zeros((N_KEYS, ntok), F32)
        for a in range(PEER_TOPK):
            cnt1 = jnp.where(rank1 == float(a), cnt_rows[a], cnt1)
        row_ref[hd, 0] = cnt1
        row_ref[hd, 1] = jnp.exp(sc[0] - sv1[0]) / zsum
        tile_ref[hd, 0] = rank2
        tile_ref[hd, 1] = jnp.exp(sc[1] - sv2[0])


def _peer_select(pq, sk):
    t = pq.shape[0]
    tl = PEER_TOK_LANES
    shape = (t // tl, PEER_HEADS, 2, N_KEYS, tl)
    spec = pl.BlockSpec((None, PEER_HEADS, 2, N_KEYS, tl), lambda i: (i, 0, 0, 0, 0))
    return pl.pallas_call(
        _peer_select_kernel,
        grid=(t // tl,),
        in_specs=[
            pl.BlockSpec((tl, pq.shape[1]), lambda i: (i, 0)),
            pl.BlockSpec(sk.shape, lambda i: (0, 0, 0)),
        ],
        out_specs=[spec, spec],
        out_shape=[jax.ShapeDtypeStruct(shape, F32), jax.ShapeDtypeStruct(shape, F32)],
        compiler_params=_cparams(("parallel",)),
        name="peer_select",
    )(pq, sk)


def _gelu(x):
    return 0.5 * x * (1.0 + lax.erf(x * np.float32(2.0 ** -0.5)))


def _peer_dense_kernel(h_ref, row_ref, tile_ref, u_ref, vt_ref, x_ref, out_ref, act_ref, g_ref, acc_ref, *, nsub):
    e = pl.program_id(1)
    keys_per_blk = PEER_EXPERT_BLK // N_KEYS
    groups = N_KEYS // SUBLANES
    vreg = (SUBLANES, PEER_TOK_LANES)

    @pl.when(e == 0)
    def _():
        acc_ref[...] = jnp.zeros_like(acc_ref)

    act_ref[...] = _gelu(_dot_nt(u_ref[...], h_ref[...]))

    def gate(kk, carry):
        i = e * keys_per_blk + kk
        row0 = pl.multiple_of(kk * N_KEYS, N_KEYS)
        for t in range(nsub):
            lanes = slice(t * PEER_TOK_LANES, (t + 1) * PEER_TOK_LANES)
            w = [None] * groups
            for hd in range(PEER_HEADS):
                cnt = jnp.broadcast_to(row_ref[t, hd, 0, pl.ds(i, 1), :], vreg)
                e1 = jnp.broadcast_to(row_ref[t, hd, 1, pl.ds(i, 1), :], vreg)
                for g in range(groups):
                    keys = slice(g * SUBLANES, (g + 1) * SUBLANES)
                    term = jnp.where(tile_ref[t, hd, 0, keys, :] < cnt, e1 * tile_ref[t, hd, 1, keys, :], 0.0)
                    w[g] = term if w[g] is None else w[g] + term
            rows = pl.ds(row0, N_KEYS)
            g_ref[rows, lanes] = (jnp.concatenate(w, axis=0) * act_ref[rows, lanes]).astype(BF16)
        return carry

    lax.fori_loop(0, keys_per_blk, gate, 0)
    acc_ref[...] += _dot(vt_ref[...], g_ref[...])

    @pl.when(e == pl.num_programs(1) - 1)
    def _():
        out_ref[...] = x_ref[...] + acc_ref[...].T


def _peer_dense(h2, sel_rows, sel_tiles, u_bf, vt_bf, x1, tb):
    t = h2.shape[0]
    nsub = tb // PEER_TOK_LANES
    eb = PEER_EXPERT_BLK
    sel_spec = pl.BlockSpec((nsub, PEER_HEADS, 2, N_KEYS, PEER_TOK_LANES), lambda i, e: (i, 0, 0, 0, 0))
    return pl.pallas_call(
        functools.partial(_peer_dense_kernel, nsub=nsub),
        grid=(t // tb, N_EXPERTS // eb),
        in_specs=[
            pl.BlockSpec((tb, D_MODEL), lambda i, e: (i, 0)),
            sel_spec,
            sel_spec,
            pl.BlockSpec((eb, D_MODEL), lambda i, e: (e, 0)),
            pl.BlockSpec((D_MODEL, eb), lambda i, e: (0, e)),
            pl.BlockSpec((tb, D_MODEL), lambda i, e: (i, 0)),
        ],
        out_specs=pl.BlockSpec((tb, D_MODEL), lambda i, e: (i, 0)),
        out_shape=jax.ShapeDtypeStruct((t, D_MODEL), F32),
        scratch_shapes=[
            pltpu.VMEM((eb, tb), F32),
            pltpu.VMEM((eb, tb), BF16),
            pltpu.VMEM((D_MODEL, tb), F32),
        ],
        compiler_params=_cparams(("parallel", "arbitrary")),
        name="peer_dense",
    )(h2, sel_rows, sel_tiles, u_bf, vt_bf, x1)


def _peer(h2, pq, x1, sk, u_bf, vt_bf):
    t = h2.shape[0]
    tb = min(512, -(-t // PEER_TOK_LANES) * PEER_TOK_LANES)
    t_pad = -(-t // tb) * tb
    if t_pad != t:
        h2 = jnp.pad(h2, ((0, t_pad - t), (0, 0)))
        x1 = jnp.pad(x1, ((0, t_pad - t), (0, 0)))
        pq = jnp.pad(pq, ((0, t_pad - t), (0, 0)))
    sel_rows, sel_tiles = _peer_select(pq, sk)
    return _peer_dense(h2, sel_rows, sel_tiles, u_bf, vt_bf, x1, tb)[:t]


def _t5_bucket(dist):
    dist = np.asarray(dist, np.int64)
    max_exact = N_BUCKETS // 2
    ratio = np.log(np.maximum(dist, 1) / max_exact) / np.log(REL_MAX_DIST / max_exact)
    large = np.minimum(max_exact + (ratio * (N_BUCKETS - max_exact)).astype(np.int64), N_BUCKETS - 1)
    return np.where(dist < max_exact, dist, large).astype(np.int32)


def _rotary_tables(pos):
    half = DK_R // 2
    ang = jnp.asarray(1.0 / (10000.0 ** np.linspace(0.0, 1.0, half)), F32)
    theta = pos.astype(F32)[:, None] * ang[None, :]
    c, s = jnp.cos(theta), jnp.sin(theta)
    return jnp.concatenate([c, c], axis=-1), jnp.concatenate([-s, s], axis=-1)


def _retention_tables(length, log_g):
    n = np.arange(length)
    diff = n[:, None] - n[None, :]
    decay = jnp.where(jnp.asarray(diff >= 0)[None],
                      jnp.exp(jnp.asarray(np.maximum(diff, 0), F32)[None] * log_g[:, None, None]), 0.0)
    xi = jnp.exp(jnp.asarray(n + 1, F32)[None, :] * log_g[:, None])
    zeta = jnp.exp(jnp.asarray(n[::-1].copy(), F32)[None, :] * log_g[:, None])
    xi = jnp.broadcast_to(xi[:, :, None], (HR, length, DV_R))
    zeta = jnp.broadcast_to(zeta[:, :, None], (HR, length, DK_R))
    return decay, xi, zeta


def _prompt_bias(bias_g, g, d):
    steps = ATTN_GROUPS[g][0] // d
    qi = np.arange(ATTN_BLK)[:, None]
    ki = np.arange(2 * ATTN_BLK)[None, :]
    m = ATTN_BLK + qi - ki
    return jnp.transpose(bias_g[:, g][_t5_bucket(np.clip(m, 0, steps) * d)], (2, 0, 1))


def _sample_bias(bias_g):
    rows, selfs = [], []
    for g, (window, d) in enumerate(ATTN_GROUPS):
        steps = window // d
        assert steps == ATTN_BLK
        j = steps - np.arange(steps)
        rows.append(bias_g[:, g][_t5_bucket(j * d)])
        selfs.append(jnp.broadcast_to(bias_g[:, g][_t5_bucket(np.zeros(1))], (8, HEADS)))
    return jnp.stack(rows), jnp.stack(selfs)


def kernel(x_prompt, x_sample, cache_kv_d1, cache_kv_d4, cache_kv_d16, state_ret, rel_bias, norm1_w, w_in, b_gate, q_norm_w, k_norm_w, w_o_attn, w_o_ret, w_out, norm2_w, peer_w_q, peer_sub_keys, peer_u, peer_v):
    b, s, _ = x_prompt.shape
    db, ds, _ = x_sample.shape
    assert ds == 1, "the sample group is one new token per sequence"
    depth = w_in.shape[0]
    gw = HEADS * HD_A
    caches = (cache_kv_d1, cache_kv_d4, cache_kv_d16)

    bias_g = rel_bias.reshape(N_BUCKETS, N_GROUPS, HEADS).astype(F32)
    log_g = jnp.asarray(np.log(1.0 - 2.0 ** (-5.0 - np.arange(HR))), F32)
    cos_p, sin_p = _rotary_tables(jnp.arange(s, dtype=jnp.int32))
    cos_s, sin_s = _rotary_tables(jnp.full((db,), PAST_LEN, jnp.int32))
    dec_p, xi_p, zeta_p = _retention_tables(RET_CHUNK, log_g)
    gam_s = _retention_tables(1, log_g)[1]
    bias_p = jnp.stack([_prompt_bias(bias_g, g, d) for g, (_, d) in enumerate(ATTN_GROUPS)])
    brow_s, bself_s = _sample_bias(bias_g)

    tm_p = 512 if (b * s) % 512 == 0 else ATTN_BLK
    xp = x_prompt.reshape(b * s, D_MODEL)
    xs = x_sample.reshape(db, D_MODEL)
    kv_p = [[] for _ in range(N_GROUPS)]
    kv_s = [[] for _ in range(N_GROUPS)]
    ret_p, ret_s = [], []

    for l in range(depth):
        w_in_bf = w_in[l].astype(BF16)
        ones = jnp.ones((COL_TILE,), F32)
        colw = jnp.stack(
            [jnp.tile(q_norm_w[l, g], HEADS) * (HD_A ** -0.5) for g in range(N_GROUPS)]
            + [jnp.tile(k_norm_w[l, g], HEADS) for g in range(N_GROUPS)]
            + [ones] * (CT_KR - CT_V) + [ones * (DK_R ** -0.5)] + [ones] * (N_COL_TILES - CT_KR - 1)
        ).reshape(N_COL_TILES, 1, COL_TILE)
        n1 = norm1_w[l].reshape(1, D_MODEL)
        n2 = norm2_w[l].reshape(1, D_MODEL)
        bg = b_gate[l].reshape(1, 2 * D_MODEL)
        woa, wor, wout = w_o_attn[l].astype(BF16), w_o_ret[l].astype(BF16), w_out[l].astype(BF16)
        wq = peer_w_q[l].astype(BF16)
        sk = peer_sub_keys[l]
        u_bf = peer_u[l].astype(BF16)
        vt_bf = peer_v[l].T.astype(BF16)

        zp = _inproj(xp, n1, w_in_bf, colw, cos_p, sin_p, tm_p, s // tm_p)
        zp3 = zp.reshape(b, s, IN_COLS)
        a_p = _attn_prompt(zp3, bias_p, b, s)
        for g, (window, d) in enumerate(ATTN_GROUPS):
            keep = min(window, s)
            k_rows = zp3[:, s - keep:, QA_W + g * gw:QA_W + (g + 1) * gw].reshape(b, keep, HEADS, HD_A)
            v_rows = zp3[:, s - keep:, 2 * QA_W + g * gw:2 * QA_W + (g + 1) * gw].reshape(b, keep, HEADS, HD_A)
            kv_p[g].append(jnp.stack([k_rows, v_rows], axis=2))
        ro, r_new = _ret_prompt(zp3, dec_p, xi_p, zeta_p, b, s)
        ret_p.append(r_new)
        x1, h2, pq = _merge(a_p, ro, zp, xp, woa, wor, wout, wq, bg, n2, 256 if (b * s) % 256 == 0 else ATTN_BLK)
        xp = _peer(h2, pq, x1, sk, u_bf, vt_bf)

        zs = _inproj(xs, n1, w_in_bf, colw, cos_s, sin_s, db, 1)
        zs3 = zs.reshape(db, 1, IN_COLS)
        qa_s, ka_s, va_s = zs3[:, :, :QA_W], zs3[:, :, QA_W:2 * QA_W], zs3[:, :, 2 * QA_W:3 * QA_W]
        c0 = 3 * QA_W
        qr_s, kr_s = zs3[:, :, c0:c0 + RQ_W], zs3[:, :, c0 + RQ_W:c0 + 2 * RQ_W]
        c1 = c0 + 2 * RQ_W
        vr_s, gr_s = zs3[:, :, c1:c1 + RV_W], zs3[:, :, c1 + RV_W:c1 + 2 * RV_W]
        a_s = _attn_sample(qa_s, ka_s, va_s, [c[l] for c in caches], brow_s, bself_s)
        for g in range(N_GROUPS):
            k_rows = ka_s[:, :, g * gw:(g + 1) * gw].reshape(db, 1, HEADS, HD_A)
            v_rows = va_s[:, :, g * gw:(g + 1) * gw].reshape(db, 1, HEADS, HD_A)
            kv_s[g].append(jnp.stack([k_rows, v_rows], axis=2))
        ro_s, r_s = _ret_sample(qr_s, kr_s, vr_s, gr_s, state_ret[l], gam_s)
        ret_s.append(r_s)
        x1s, h2s, pqs = _merge(a_s.reshape(db, gw), ro_s.reshape(db, RV_W), zs, xs, woa, wor, wout, wq, bg, n2, db)
        xs = _peer(h2s, pqs, x1s, sk, u_bf, vt_bf)

    return (xp.reshape(b, s, D_MODEL), xs.reshape(db, 1, D_MODEL),
            jnp.stack(kv_p[0]), jnp.stack(kv_p[1]), jnp.stack(kv_p[2]), jnp.stack(ret_p),
            jnp.stack(kv_s[0]), jnp.stack(kv_s[1]), jnp.stack(kv_s[2]), jnp.stack(ret_s))
```

```python
import functools

import numpy as np
import jax
import jax.numpy as jnp
from jax import lax
from jax.experimental import pallas as pl
from jax.experimental.pallas import tpu as pltpu

F32 = jnp.float32
BF16 = jnp.bfloat16

D_MODEL = 1024
PAST_LEN = 16384
ATTN_GROUPS = ((128, 1), (512, 4), (2048, 16))
N_GROUPS = 3
HEADS = 4
HD_A = 128
ATTN_BLK = 128
ATTN_ROWS = 2048
N_BUCKETS = 32
REL_MAX_DIST = 2048
HR = 4
DK_R = 128
DV_R = 256
RET_CHUNK = 128
N_KEYS = 128
N_EXPERTS = N_KEYS * N_KEYS
PEER_HEADS = 8
PEER_TOPK = 16
EPS = 1e-6
NEG = -1e30

QA_W = N_GROUPS * HEADS * HD_A
RQ_W = HR * DK_R
RV_W = HR * DV_R
IN_COLS = 3 * QA_W + 2 * RQ_W + 2 * RV_W + 2 * D_MODEL
COL_TILE = 512
N_COL_TILES = IN_COLS // COL_TILE
CT_K = QA_W // COL_TILE
CT_V = 2 * QA_W // COL_TILE
CT_QR = 3 * QA_W // COL_TILE
CT_KR = CT_QR + 1
CT_VR = CT_KR + 1
CT_GR = CT_VR + 2
CT_GATE = CT_GR + 2

LANES = 128
SUBLANES = 8
VMEM_LIMIT = 48 * 1024 * 1024

PEER_TOK_LANES = 128
PEER_EXPERT_BLK = 1024


def _dot(a, b, precision=None):
    return jnp.dot(a, b, preferred_element_type=F32, precision=precision)


def _dot_nt(a, b, precision=None):
    return lax.dot_general(a, b, (((1,), (1,)), ((), ())), preferred_element_type=F32, precision=precision)


def _cparams(sem):
    return pltpu.CompilerParams(dimension_semantics=sem, vmem_limit_bytes=VMEM_LIMIT)


def _inproj_kernel(x_ref, n1_ref, w_ref, colw_ref, cos_ref, sin_ref, z_ref, hn_ref):
    j = pl.program_id(1)

    @pl.when(j == 0)
    def _():
        x = x_ref[...]
        ms = jnp.mean(x * x, axis=-1, keepdims=True)
        hn_ref[...] = (x * lax.rsqrt(ms + EPS) * n1_ref[...]).astype(BF16)

    y = _dot(hn_ref[...], w_ref[...])
    cw = colw_ref[...]

    @pl.when(j < CT_V)
    def _():
        for h in range(HEADS):
            sl = slice(h * HD_A, (h + 1) * HD_A)
            yh = y[:, sl]
            ms = jnp.mean(yh * yh, axis=-1, keepdims=True)
            z_ref[:, sl] = yh * lax.rsqrt(ms + EPS) * cw[:, sl]

    @pl.when((j == CT_QR) | (j == CT_KR))
    def _():
        c = cos_ref[...]
        s = sin_ref[...]
        for h in range(HR):
            sl = slice(h * DK_R, (h + 1) * DK_R)
            yh = y[:, sl]
            z_ref[:, sl] = (yh * c + pltpu.roll(yh, DK_R // 2, 1) * s) * cw[:, sl]

    @pl.when(((j >= CT_V) & (j < CT_QR)) | (j >= CT_VR))
    def _():
        z_ref[...] = y


def _inproj(x, n1, w_bf, colw, cos_t, sin_t, tm, pos_blocks):
    t = x.shape[0]
    return pl.pallas_call(
        _inproj_kernel,
        grid=(t // tm, N_COL_TILES),
        in_specs=[
            pl.BlockSpec((tm, D_MODEL), lambda i, j: (i, 0)),
            pl.BlockSpec((1, D_MODEL), lambda i, j: (0, 0)),
            pl.BlockSpec((D_MODEL, COL_TILE), lambda i, j: (0, j)),
            pl.BlockSpec((None, 1, COL_TILE), lambda i, j: (j, 0, 0)),
            pl.BlockSpec((tm, DK_R), lambda i, j: (i % pos_blocks, 0)),
            pl.BlockSpec((tm, DK_R), lambda i, j: (i % pos_blocks, 0)),
        ],
        out_specs=pl.BlockSpec((tm, COL_TILE), lambda i, j: (i, j)),
        out_shape=jax.ShapeDtypeStruct((t, IN_COLS), F32),
        scratch_shapes=[pltpu.VMEM((tm, D_MODEL), BF16)],
        compiler_params=_cparams(("parallel", "arbitrary")),
        name="inproj",
    )(x, n1, w_bf, colw, cos_t, sin_t)


def _attn_tile(q, kp, kc, vp, vc, bias_p, bias_c, prev_ok):
    qi = lax.broadcasted_iota(jnp.int32, (ATTN_BLK, ATTN_BLK), 0)
    ki = lax.broadcasted_iota(jnp.int32, (ATTN_BLK, ATTN_BLK), 1)
    qb = q.astype(BF16)
    sp = _dot_nt(qb, kp.astype(BF16)) + bias_p
    sc = _dot_nt(qb, kc.astype(BF16)) + bias_c
    sp = jnp.where(jnp.logical_and(ki >= qi, prev_ok), sp, NEG)
    sc = jnp.where(ki <= qi, sc, NEG)
    m = jnp.maximum(jnp.max(sp, axis=-1, keepdims=True), jnp.max(sc, axis=-1, keepdims=True))
    pp = jnp.exp(sp - m)
    pc = jnp.exp(sc - m)
    den = jnp.sum(pp, axis=-1, keepdims=True) + jnp.sum(pc, axis=-1, keepdims=True)
    acc = _dot(pp.astype(BF16), vp.astype(BF16)) + _dot(pc.astype(BF16), vc.astype(BF16))
    return acc / den, m + jnp.log(den)


def _attn_prompt_kernel(*refs):
    q_refs, kc_refs, vc_refs, kp_refs, vp_refs = (refs[3 * n:3 * n + 3] for n in range(5))
    bias_ref, a_ref, o_scr, l_scr = refs[15:]
    i = pl.program_id(2)
    for g, (_, d) in enumerate(ATTN_GROUPS):
        span = ATTN_BLK * d
        nblk = ATTN_ROWS // span
        q_ref, kc_ref, vc_ref, kp_ref, vp_ref = q_refs[g], kc_refs[g], vc_refs[g], kp_refs[g], vp_refs[g]

        def tile(idx, carry, g=g, d=d, span=span, nblk=nblk,
                 q_ref=q_ref, kc_ref=kc_ref, vc_ref=vc_ref, kp_ref=kp_ref, vp_ref=vp_ref):
            r = idx % d
            c = idx // d
            rows = pl.ds(c * span + r, ATTN_BLK, stride=d)
            edge = pl.ds(r, ATTN_BLK, stride=d)
            if nblk == 1:
                kp, vp = kp_ref[edge, :], vp_ref[edge, :]
            else:
                inner = pl.ds(jnp.maximum(c - 1, 0) * span + r, ATTN_BLK, stride=d)
                kp = jnp.where(c == 0, kp_ref[edge, :], kc_ref[inner, :])
                vp = jnp.where(c == 0, vp_ref[edge, :], vc_ref[inner, :])
            prev_ok = jnp.logical_or(c > 0, i > 0)
            o, lse = _attn_tile(q_ref[rows, :], kp, kc_ref[rows, :], vp, vc_ref[rows, :],
                                bias_ref[g, :, :ATTN_BLK], bias_ref[g, :, ATTN_BLK:], prev_ok)
            o_scr[g, rows, :] = o
            l_scr[g, rows, :] = jnp.broadcast_to(lse, (ATTN_BLK, HD_A))
            return carry

        lax.fori_loop(0, d * nblk, tile, 0)

    def combine(c, carry):
        rows = pl.ds(pl.multiple_of(c * ATTN_BLK, ATTN_BLK), ATTN_BLK)
        la, lb, lc = l_scr[0, rows, :], l_scr[1, rows, :], l_scr[2, rows, :]
        m = jnp.maximum(jnp.maximum(la, lb), lc)
        ea, eb, ec = jnp.exp(la - m), jnp.exp(lb - m), jnp.exp(lc - m)
        a_ref[rows, :] = (ea * o_scr[0, rows, :] + eb * o_scr[1, rows, :] + ec * o_scr[2, rows, :]) / (ea + eb + ec)
        return carry

    lax.fori_loop(0, ATTN_ROWS // ATTN_BLK, combine, 0)


def _attn_prompt(z3, bias, b, s):
    col = lambda base, g: (lambda bb, h, i, c0=(base + g * HEADS * HD_A) // HD_A: (bb, i, c0 + h))

    def prev_spec(base, g, d):
        span = ATTN_BLK * d
        nblk = ATTN_ROWS // span
        c0 = (base + g * HEADS * HD_A) // HD_A
        return pl.BlockSpec((None, span, HD_A), lambda bb, h, i: (bb, jnp.maximum(i * nblk - 1, 0), c0 + h))

    cur = lambda base, g: pl.BlockSpec((None, ATTN_ROWS, HD_A), col(base, g))
    groups = range(N_GROUPS)
    in_specs = ([cur(0, g) for g in groups] + [cur(QA_W, g) for g in groups] + [cur(2 * QA_W, g) for g in groups]
                + [prev_spec(QA_W, g, d) for g, (_, d) in enumerate(ATTN_GROUPS)]
                + [prev_spec(2 * QA_W, g, d) for g, (_, d) in enumerate(ATTN_GROUPS)]
                + [pl.BlockSpec((N_GROUPS, None, ATTN_BLK, 2 * ATTN_BLK), lambda bb, h, i: (0, h, 0, 0))])
    a = pl.pallas_call(
        _attn_prompt_kernel,
        grid=(b, HEADS, s // ATTN_ROWS),
        in_specs=in_specs,
        out_specs=pl.BlockSpec((None, ATTN_ROWS, HD_A), lambda bb, h, i: (bb, i, h)),
        out_shape=jax.ShapeDtypeStruct((b, s, HEADS * HD_A), F32),
        scratch_shapes=[pltpu.VMEM((N_GROUPS, ATTN_ROWS, HD_A), F32), pltpu.VMEM((N_GROUPS, ATTN_ROWS, HD_A), F32)],
        compiler_params=_cparams(("parallel", "parallel", "arbitrary")),
        name="attn_prompt",
    )(*([z3] * 15), bias)
    return a.reshape(b * s, HEADS * HD_A)


def _attn_sample_kernel(q_ref, k_ref, v_ref, c0_ref, c1_ref, c2_ref, brow_ref, bself_ref, a_ref):
    caches = (c0_ref, c1_ref, c2_ref)
    gw = HEADS * HD_A
    for h in range(HEADS):
        outs, lses = [], []
        for g in range(N_GROUPS):
            col = g * gw + h * HD_A
            q = q_ref[:, col:col + HD_A]
            kn = k_ref[:, col:col + HD_A]
            vn = v_ref[:, col:col + HD_A]
            kc = caches[g][:, 0, h, :]
            vc = caches[g][:, 1, h, :]
            sc = jnp.sum(kc * q, axis=-1, keepdims=True) + brow_ref[g, :, h:h + 1]
            s0 = jnp.sum(kn * q, axis=-1, keepdims=True) + bself_ref[g, :, h:h + 1][0:1]
            m = jnp.maximum(jnp.max(sc, axis=0, keepdims=True), s0)
            e = jnp.exp(sc - m)
            e0 = jnp.exp(s0 - m)
            den = jnp.sum(e, axis=0, keepdims=True) + e0
            outs.append((jnp.sum(e * vc, axis=0, keepdims=True) + e0 * vn) / den)
            lses.append(m + jnp.log(den))
        mm = jnp.maximum(jnp.maximum(lses[0], lses[1]), lses[2])
        ws = [jnp.exp(l - mm) for l in lses]
        wsum = ws[0] + ws[1] + ws[2]
        a_ref[:, h * HD_A:(h + 1) * HD_A] = (ws[0] * outs[0] + ws[1] * outs[1] + ws[2] * outs[2]) / wsum


def _attn_sample(qa, ka, va, caches, layer, brow, bself):
    db = qa.shape[0]
    gw = HEADS * HD_A
    cviews, cspecs = [], []
    for c, (window, d) in zip(caches, ATTN_GROUPS):
        assert c.shape[2] == window, "sample attention expects a full window of cached rows"
        cviews.append(c.reshape(c.shape[0], db, window // d, d, 2, HEADS, HD_A))
        cspecs.append(pl.BlockSpec((None, None, ATTN_BLK, None, 2, HEADS, HD_A), lambda b: (layer, b, 0, 0, 0, 0, 0)))
    tok = pl.BlockSpec((None, 1, QA_W), lambda b: (b, 0, 0))
    return pl.pallas_call(
        _attn_sample_kernel,
        grid=(db,),
        in_specs=[tok, tok, tok] + cspecs + [
            pl.BlockSpec((N_GROUPS, ATTN_BLK, HEADS), lambda b: (0, 0, 0)),
            pl.BlockSpec((N_GROUPS, 8, HEADS), lambda b: (0, 0, 0)),
        ],
        out_specs=pl.BlockSpec((None, 1, gw), lambda b: (b, 0, 0)),
        out_shape=jax.ShapeDtypeStruct((db, 1, gw), F32),
        compiler_params=_cparams(("parallel",)),
        name="attn_sample",
    )(qa, ka, va, *cviews, brow, bself)


def _ret_epilogue(o, gate):
    on = o * lax.rsqrt(jnp.mean(o * o, axis=-1, keepdims=True) + EPS)
    return on * (gate / (1.0 + jnp.exp(-gate)))


def _ret_prompt_kernel(q_ref, k_ref, va_ref, vb_ref, ga_ref, gb_ref, dec_ref, xi_ref, zeta_ref, o_ref, rout_ref, r_scr):
    c = pl.program_id(1)
    heads_per_tile = COL_TILE // DV_R

    @pl.when(c == 0)
    def _():
        r_scr[...] = jnp.zeros_like(r_scr)

    for h in range(HR):
        qs = slice(h * DK_R, (h + 1) * DK_R)
        vs = slice(h * DV_R, (h + 1) * DV_R)
        ts = slice((h % heads_per_tile) * DV_R, (h % heads_per_tile + 1) * DV_R)
        v_ref, g_ref = ((va_ref, ga_ref), (vb_ref, gb_ref))[h // heads_per_tile]
        q = q_ref[:, qs]
        k = k_ref[:, qs]
        qb = q.astype(BF16)
        vb = v_ref[:, ts].astype(BF16)
        r = r_scr[h]
        xi = xi_ref[h]
        s = _dot_nt(qb, k.astype(BF16)) * dec_ref[h]
        o = _dot(s.astype(BF16), vb) + _dot(qb, r.astype(BF16)) * xi
        kz = (k * zeta_ref[h]).T
        r_scr[h] = xi[RET_CHUNK - 1:RET_CHUNK, :] * r + _dot(kz.astype(BF16), vb)
        o_ref[:, vs] = _ret_epilogue(o, g_ref[:, ts])

    @pl.when(c == pl.num_programs(1) - 1)
    def _():
        rout_ref[...] = r_scr[...]


def _ret_prompt(z3, dec, xi, zeta, b, s):
    nc = s // RET_CHUNK
    qk_w = HR * DK_R
    tab = lambda w: pl.BlockSpec((HR, RET_CHUNK, w), lambda bb, c: (0, 0, 0))
    o, r = pl.pallas_call(
        _ret_prompt_kernel,
        grid=(b, nc),
        in_specs=[
            pl.BlockSpec((None, RET_CHUNK, qk_w), lambda bb, c: (bb, c, CT_QR)),
            pl.BlockSpec((None, RET_CHUNK, qk_w), lambda bb, c: (bb, c, CT_KR)),
            pl.BlockSpec((None, RET_CHUNK, COL_TILE), lambda bb, c: (bb, c, CT_VR)),
            pl.BlockSpec((None, RET_CHUNK, COL_TILE), lambda bb, c: (bb, c, CT_VR + 1)),
            pl.BlockSpec((None, RET_CHUNK, COL_TILE), lambda bb, c: (bb, c, CT_GR)),
            pl.BlockSpec((None, RET_CHUNK, COL_TILE), lambda bb, c: (bb, c, CT_GR + 1)),
            tab(RET_CHUNK), tab(DV_R), tab(DK_R),
        ],
        out_specs=[
            pl.BlockSpec((None, RET_CHUNK, RV_W), lambda bb, c: (bb, c, 0)),
            pl.BlockSpec((None, HR, DK_R, DV_R), lambda bb, c: (bb, 0, 0, 0)),
        ],
        out_shape=[jax.ShapeDtypeStruct((b, s, RV_W), F32), jax.ShapeDtypeStruct((b, HR, DK_R, DV_R), F32)],
        scratch_shapes=[pltpu.VMEM((HR, DK_R, DV_R), F32)],
        compiler_params=_cparams(("parallel", "arbitrary")),
        name="ret_prompt",
    )(z3, z3, z3, z3, z3, z3, dec, xi, zeta)
    return o.reshape(b * s, RV_W), r


def _ret_sample_kernel(q_ref, k_ref, v_ref, g_ref, r0_ref, gam_ref, o_ref, rout_ref):
    eye = (lax.broadcasted_iota(jnp.int32, (DK_R, DK_R), 0) == lax.broadcasted_iota(jnp.int32, (DK_R, DK_R), 1))
    for h in range(HR):
        qs = slice(h * DK_R, (h + 1) * DK_R)
        vs = slice(h * DV_R, (h + 1) * DV_R)
        q = q_ref[:, qs]
        k = k_ref[:, qs]
        v = v_ref[:, vs]
        r = r0_ref[h]
        gam = gam_ref[h]
        qcol = jnp.sum(jnp.where(eye, jnp.broadcast_to(q, (DK_R, DK_R)), 0.0), axis=-1, keepdims=True)
        kcol = jnp.sum(jnp.where(eye, jnp.broadcast_to(k, (DK_R, DK_R)), 0.0), axis=-1, keepdims=True)
        qk = jnp.sum(q * k, axis=-1, keepdims=True)
        o = qk * v + jnp.sum(qcol * r, axis=0, keepdims=True) * gam
        rout_ref[h] = gam * r + kcol * v
        o_ref[:, vs] = _ret_epilogue(o, g_ref[:, vs])


def _ret_sample(qr, kr, vr, gr, state, layer, gam):
    db = qr.shape[0]
    qk_w = HR * DK_R
    return pl.pallas_call(
        _ret_sample_kernel,
        grid=(db,),
        in_specs=[
            pl.BlockSpec((None, 1, qk_w), lambda b: (b, 0, 0)),
            pl.BlockSpec((None, 1, qk_w), lambda b: (b, 0, 0)),
            pl.BlockSpec((None, 1, RV_W), lambda b: (b, 0, 0)),
            pl.BlockSpec((None, 1, RV_W), lambda b: (b, 0, 0)),
            pl.BlockSpec((None, None, HR, DK_R, DV_R), lambda b: (layer, b, 0, 0, 0)),
            pl.BlockSpec((HR, 1, DV_R), lambda b: (0, 0, 0)),
        ],
        out_specs=[
            pl.BlockSpec((None, 1, RV_W), lambda b: (b, 0, 0)),
            pl.BlockSpec((None, HR, DK_R, DV_R), lambda b: (b, 0, 0, 0)),
        ],
        out_shape=[jax.ShapeDtypeStruct((db, 1, RV_W), F32), jax.ShapeDtypeStruct((db, HR, DK_R, DV_R), F32)],
        compiler_params=_cparams(("parallel",)),
        name="ret_sample",
    )(qr, kr, vr, gr, state, gam)


def _merge_kernel(a_ref, ro_ref, ga0, ga1, gb0, gb1, x_ref, woa_ref, wor_ref, wout_ref, wq_ref, bg_ref, n2_ref,
                  x1_ref, h2_ref, pq_ref):
    a = a_ref[...]
    ba = _dot(a.astype(BF16), woa_ref[...])
    br = _dot(ro_ref[...].astype(BF16), wor_ref[...])
    ga = jnp.concatenate([ga0[...], ga1[...]], axis=-1) + bg_ref[:, :D_MODEL]
    gb = jnp.concatenate([gb0[...], gb1[...]], axis=-1) + bg_ref[:, D_MODEL:]
    mixed = ba / (1.0 + jnp.exp(-ga)) + br / (1.0 + jnp.exp(-gb))
    x1 = x_ref[...] + _dot(mixed.astype(BF16), wout_ref[...])
    x1_ref[...] = x1
    h2 = (x1 * lax.rsqrt(jnp.mean(x1 * x1, axis=-1, keepdims=True) + EPS) * n2_ref[...]).astype(BF16)
    h2_ref[...] = h2
    pq_ref[...] = _dot(h2, wq_ref[...])


def _merge(a, ro, z, x, woa, wor, wout, wq, bg, n2, tm):
    t = x.shape[0]
    gw = HEADS * HD_A
    row = lambda w, c=0: pl.BlockSpec((tm, w), lambda i, c=c: (i, c))
    full = lambda a: pl.BlockSpec(a.shape, lambda i: (0, 0))
    in_specs = [
        row(gw), row(RV_W),
        row(COL_TILE, CT_GATE), row(COL_TILE, CT_GATE + 1), row(COL_TILE, CT_GATE + 2), row(COL_TILE, CT_GATE + 3),
        row(D_MODEL),
        full(woa), full(wor), full(wout), full(wq), full(bg), full(n2),
    ]
    return pl.pallas_call(
        _merge_kernel,
        grid=(t // tm,),
        in_specs=in_specs,
        out_specs=[row(D_MODEL), row(D_MODEL), row(wq.shape[1])],
        out_shape=[jax.ShapeDtypeStruct((t, D_MODEL), F32), jax.ShapeDtypeStruct((t, D_MODEL), BF16),
                   jax.ShapeDtypeStruct((t, wq.shape[1]), F32)],
        compiler_params=_cparams(("parallel",)),
        name="merge",
    )(a, ro, z, z, z, z, x, woa, wor, wout, wq, bg, n2)


def _topk_rank(s, break_ties):
    key = lax.broadcasted_iota(jnp.int32, s.shape, 0).astype(F32)
    rank = jnp.full(s.shape, float(PEER_TOPK), F32)
    vals = []
    for k in range(PEER_TOPK):
        m = jnp.max(s, axis=0, keepdims=True)
        hit = s == m
        if break_ties:
            hit = key == jnp.min(jnp.where(hit, key, float(N_KEYS)), axis=0, keepdims=True)
        rank = jnp.where(hit, float(k), rank)
        s = jnp.where(hit, -jnp.inf, s)
        vals.append(m)
    return rank, vals


def _exactly_16(picked):
    return jnp.where(jnp.sum(picked, axis=0, keepdims=True) != float(PEER_TOPK), 1.0, 0.0)


def _peer_select_head(pq_ref, sk_ref, row_ref, tile_ref, hd, break_ties):
    ntok = pq_ref.shape[0]
    hi = lax.Precision.HIGHEST
    sub = lax.broadcasted_iota(jnp.int32, (SUBLANES, ntok), 0).astype(F32)
    flat = jnp.concatenate([sub, 8.0 + sub, 16.0 + sub, 32.0 + sub, 48.0 + sub,
                            (8.0 + sub) * 16.0, sub * 16.0, sub * 16.0 + 1.0, sub * 16.0 + 2.0], axis=0)
    ninf = jnp.full((SUBLANES, ntok), -jnp.inf, F32)
    sc = []
    for p in range(2):
        col = (hd * 2 + p) * N_KEYS
        sc.append(_dot_nt(sk_ref[p], pq_ref[:, col:col + N_KEYS], hi))
    rank1, sv1 = _topk_rank(sc[0], break_ties)
    rank2, sv2 = _topk_rank(sc[1], break_ties)
    a_lo, a_hi = jnp.concatenate(sv1[:8], axis=0), jnp.concatenate(sv1[8:], axis=0)
    b_lo, b_hi = jnp.concatenate(sv2[:8], axis=0), jnp.concatenate(sv2[8:], axis=0)
    cand = jnp.concatenate([
        sv1[0] + b_lo, sv1[0] + b_hi, sv1[1] + b_lo,
        jnp.where(sub < 5.0, sv1[2] + b_lo, ninf),
        jnp.where(sub < 4.0, sv1[3] + b_lo, ninf),
        a_hi + sv2[0],
        jnp.where(sub >= 4.0, a_lo + sv2[0], ninf),
        jnp.where(sub >= 4.0, a_lo + sv2[1], ninf),
        jnp.where(sub == 4.0, a_lo + sv2[2], ninf),
    ], axis=0)
    work = cand
    chosen = jnp.zeros(cand.shape, F32)
    for _ in range(PEER_TOPK):
        m = jnp.max(work, axis=0, keepdims=True)
        hit = work == m
        if break_ties:
            hit = flat == jnp.min(jnp.where(hit, flat, float(PEER_TOPK * PEER_TOPK)), axis=0, keepdims=True)
        chosen = jnp.where(hit, 1.0, chosen)
        work = jnp.where(hit, -jnp.inf, work)
    top = sv1[0] + sv2[0]
    zsum = jnp.sum(chosen * jnp.exp(cand - top), axis=0, keepdims=True)
    cnt_rows = [jnp.sum(chosen[0:16], axis=0, keepdims=True)]
    cnt_rows += [jnp.sum(chosen[8 * g:8 * g + 8], axis=0, keepdims=True) for g in (2, 3, 4)]
    cnt_mid = chosen[48:56] + chosen[56:64] + chosen[64:72]
    cnt_rows += [cnt_mid[a:a + 1] for a in range(4, 8)]
    cnt_rows += [chosen[40 + a:41 + a] for a in range(8)]
    cnt1 = jnp.zeros((N_KEYS, ntok), F32)
    for a in range(PEER_TOPK):
        cnt1 = jnp.where(rank1 == float(a), cnt_rows[a], cnt1)
    row_ref[hd, 0] = cnt1
    row_ref[hd, 1] = jnp.exp(sc[0] - sv1[0]) / zsum
    tile_ref[hd, 0] = rank2
    tile_ref[hd, 1] = jnp.exp(sc[1] - sv2[0])
    if break_ties:
        return None
    in_top = lambda rank: jnp.where(rank < float(PEER_TOPK), 1.0, 0.0)
    return _exactly_16(in_top(rank1)) + _exactly_16(in_top(rank2)) + _exactly_16(chosen)


def _peer_select_kernel(pq_ref, sk_ref, row_ref, tile_ref):
    ties = None
    for hd in range(PEER_HEADS):
        t = _peer_select_head(pq_ref, sk_ref, row_ref, tile_ref, hd, break_ties=False)
        ties = t if ties is None else ties + t

    @pl.when(jnp.max(ties) > 0.0)
    def _():
        for hd in range(PEER_HEADS):
            _peer_select_head(pq_ref, sk_ref, row_ref, tile_ref, hd, break_ties=True)


def _peer_select(pq, sk):
    t = pq.shape[0]
    tl = PEER_TOK_LANES
    shape = (t // tl, PEER_HEADS, 2, N_KEYS, tl)
    spec = pl.BlockSpec((None, PEER_HEADS, 2, N_KEYS, tl), lambda i: (i, 0, 0, 0, 0))
    return pl.pallas_call(
        _peer_select_kernel,
        grid=(t // tl,),
        in_specs=[
            pl.BlockSpec((tl, pq.shape[1]), lambda i: (i, 0)),
            pl.BlockSpec(sk.shape, lambda i: (0, 0, 0)),
        ],
        out_specs=[spec, spec],
        out_shape=[jax.ShapeDtypeStruct(shape, F32), jax.ShapeDtypeStruct(shape, F32)],
        compiler_params=_cparams(("parallel",)),
        name="peer_select",
    )(pq, sk)


def _gelu(x):
    return 0.5 * x * (1.0 + lax.erf(x * np.float32(2.0 ** -0.5)))


def _peer_dense_kernel(h_ref, row_ref, tile_ref, u_ref, vt_ref, x_ref, out_ref, act_ref, g_ref, acc_ref, *, nsub):
    e = pl.program_id(1)
    keys_per_blk = PEER_EXPERT_BLK // N_KEYS
    groups = N_KEYS // SUBLANES
    vreg = (SUBLANES, PEER_TOK_LANES)

    @pl.when(e == 0)
    def _():
        acc_ref[...] = jnp.zeros_like(acc_ref)

    act_ref[...] = _gelu(_dot_nt(u_ref[...], h_ref[...]))

    def gate(kk, carry):
        i = e * keys_per_blk + kk
        row0 = pl.multiple_of(kk * N_KEYS, N_KEYS)
        for t in range(nsub):
            lanes = slice(t * PEER_TOK_LANES, (t + 1) * PEER_TOK_LANES)
            w = [None] * groups
            for hd in range(PEER_HEADS):
                cnt = jnp.broadcast_to(row_ref[t, hd, 0, pl.ds(i, 1), :], vreg)
                e1 = jnp.broadcast_to(row_ref[t, hd, 1, pl.ds(i, 1), :], vreg)
                for g in range(groups):
                    keys = slice(g * SUBLANES, (g + 1) * SUBLANES)
                    term = jnp.where(tile_ref[t, hd, 0, keys, :] < cnt, e1 * tile_ref[t, hd, 1, keys, :], 0.0)
                    w[g] = term if w[g] is None else w[g] + term
            rows = pl.ds(row0, N_KEYS)
            g_ref[rows, lanes] = (jnp.concatenate(w, axis=0) * act_ref[rows, lanes]).astype(BF16)
        return carry

    lax.fori_loop(0, keys_per_blk, gate, 0)
    acc_ref[...] += _dot(vt_ref[...], g_ref[...])

    @pl.when(e == pl.num_programs(1) - 1)
    def _():
        out_ref[...] = x_ref[...] + acc_ref[...].T


def _peer_dense(h2, sel_rows, sel_tiles, u_bf, vt_bf, x1, tb):
    t = h2.shape[0]
    nsub = tb // PEER_TOK_LANES
    eb = PEER_EXPERT_BLK
    sel_spec = pl.BlockSpec((nsub, PEER_HEADS, 2, N_KEYS, PEER_TOK_LANES), lambda i, e: (i, 0, 0, 0, 0))
    return pl.pallas_call(
        functools.partial(_peer_dense_kernel, nsub=nsub),
        grid=(t // tb, N_EXPERTS // eb),
        in_specs=[
            pl.BlockSpec((tb, D_MODEL), lambda i, e: (i, 0)),
            sel_spec,
            sel_spec,
            pl.BlockSpec((eb, D_MODEL), lambda i, e: (e, 0)),
            pl.BlockSpec((D_MODEL, eb), lambda i, e: (0, e)),
            pl.BlockSpec((tb, D_MODEL), lambda i, e: (i, 0)),
        ],
        out_specs=pl.BlockSpec((tb, D_MODEL), lambda i, e: (i, 0)),
        out_shape=jax.ShapeDtypeStruct((t, D_MODEL), F32),
        scratch_shapes=[
            pltpu.VMEM((eb, tb), F32),
            pltpu.VMEM((eb, tb), BF16),
            pltpu.VMEM((D_MODEL, tb), F32),
        ],
        compiler_params=_cparams(("parallel", "arbitrary")),
        name="peer_dense",
    )(h2, sel_rows, sel_tiles, u_bf, vt_bf, x1)


def _peer(h2, pq, x1, sk, u_bf, vt_bf):
    t = h2.shape[0]
    tb = min(512, -(-t // PEER_TOK_LANES) * PEER_TOK_LANES)
    t_pad = -(-t // tb) * tb
    if t_pad != t:
        h2 = jnp.pad(h2, ((0, t_pad - t), (0, 0)))
        x1 = jnp.pad(x1, ((0, t_pad - t), (0, 0)))
        pq = jnp.pad(pq, ((0, t_pad - t), (0, 0)))
    sel_rows, sel_tiles = _peer_select(pq, sk)
    return _peer_dense(h2, sel_rows, sel_tiles, u_bf, vt_bf, x1, tb)[:t]


def _t5_bucket(dist):
    dist = np.asarray(dist, np.int64)
    max_exact = N_BUCKETS // 2
    ratio = np.log(np.maximum(dist, 1) / max_exact) / np.log(REL_MAX_DIST / max_exact)
    large = np.minimum(max_exact + (ratio * (N_BUCKETS - max_exact)).astype(np.int64), N_BUCKETS - 1)
    return np.where(dist < max_exact, dist, large).astype(np.int32)


def _rotary_tables(pos):
    half = DK_R // 2
    ang = jnp.asarray(1.0 / (10000.0 ** np.linspace(0.0, 1.0, half)), F32)
    theta = pos.astype(F32)[:, None] * ang[None, :]
    c, s = jnp.cos(theta), jnp.sin(theta)
    return jnp.concatenate([c, c], axis=-1), jnp.concatenate([-s, s], axis=-1)


def _retention_tables(length, log_g):
    n = np.arange(length)
    diff = n[:, None] - n[None, :]
    decay = jnp.where(jnp.asarray(diff >= 0)[None],
                      jnp.exp(jnp.asarray(np.maximum(diff, 0), F32)[None] * log_g[:, None, None]), 0.0)
    xi = jnp.exp(jnp.asarray(n + 1, F32)[None, :] * log_g[:, None])
    zeta = jnp.exp(jnp.asarray(n[::-1].copy(), F32)[None, :] * log_g[:, None])
    xi = jnp.broadcast_to(xi[:, :, None], (HR, length, DV_R))
    zeta = jnp.broadcast_to(zeta[:, :, None], (HR, length, DK_R))
    return decay, xi, zeta


def _prompt_bias(bias_g, g, d):
    steps = ATTN_GROUPS[g][0] // d
    assert steps == ATTN_BLK
    per_dist = bias_g[:, g][_t5_bucket(np.arange(steps + 1) * d)]
    n = 3 * ATTN_BLK
    diag = per_dist[np.clip(2 * ATTN_BLK - 1 - np.arange(n), 0, steps)].T
    rows = jnp.tile(diag, (1, ATTN_BLK + 1))[:, :ATTN_BLK * (n + 1)].reshape(HEADS, ATTN_BLK, n + 1)
    return rows[:, ::-1, :2 * ATTN_BLK]


def _sample_bias(bias_g):
    rows, selfs = [], []
    for g, (window, d) in enumerate(ATTN_GROUPS):
        steps = window // d
        assert steps == ATTN_BLK
        j = steps - np.arange(steps)
        rows.append(bias_g[:, g][_t5_bucket(j * d)])
        selfs.append(jnp.broadcast_to(bias_g[:, g][_t5_bucket(np.zeros(1))], (8, HEADS)))
    return jnp.stack(rows), jnp.stack(selfs)


def kernel(x_prompt, x_sample, cache_kv_d1, cache_kv_d4, cache_kv_d16, state_ret, rel_bias, norm1_w, w_in, b_gate, q_norm_w, k_norm_w, w_o_attn, w_o_ret, w_out, norm2_w, peer_w_q, peer_sub_keys, peer_u, peer_v):
    b, s, _ = x_prompt.shape
    db, ds, _ = x_sample.shape
    assert ds == 1, "the sample group is one new token per sequence"
    depth = w_in.shape[0]
    gw = HEADS * HD_A
    caches = (cache_kv_d1, cache_kv_d4, cache_kv_d16)

    bias_g = rel_bias.reshape(N_BUCKETS, N_GROUPS, HEADS).astype(F32)
    log_g = jnp.asarray(np.log(1.0 - 2.0 ** (-5.0 - np.arange(HR))), F32)
    cos_p, sin_p = _rotary_tables(jnp.arange(s, dtype=jnp.int32))
    cos_s, sin_s = _rotary_tables(jnp.full((db,), PAST_LEN, jnp.int32))
    dec_p, xi_p, zeta_p = _retention_tables(RET_CHUNK, log_g)
    gam_s = _retention_tables(1, log_g)[1]
    bias_p = jnp.stack([_prompt_bias(bias_g, g, d) for g, (_, d) in enumerate(ATTN_GROUPS)])
    brow_s, bself_s = _sample_bias(bias_g)

    tm_p = next(t for t in (1024, 512, ATTN_BLK) if s % t == 0)
    xp = x_prompt.reshape(b * s, D_MODEL)
    xs = x_sample.reshape(db, D_MODEL)
    kv_p = [[] for _ in range(N_GROUPS)]
    kv_s = [[] for _ in range(N_GROUPS)]
    ret_p, ret_s = [], []

    for l in range(depth):
        w_in_bf = w_in[l].astype(BF16)
        ones = jnp.ones((COL_TILE,), F32)
        colw = jnp.stack(
            [jnp.tile(q_norm_w[l, g], HEADS) * (HD_A ** -0.5) for g in range(N_GROUPS)]
            + [jnp.tile(k_norm_w[l, g], HEADS) for g in range(N_GROUPS)]
            + [ones] * (CT_KR - CT_V) + [ones * (DK_R ** -0.5)] + [ones] * (N_COL_TILES - CT_KR - 1)
        ).reshape(N_COL_TILES, 1, COL_TILE)
        n1 = norm1_w[l].reshape(1, D_MODEL)
        n2 = norm2_w[l].reshape(1, D_MODEL)
        bg = b_gate[l].reshape(1, 2 * D_MODEL)
        woa, wor, wout = w_o_attn[l].astype(BF16), w_o_ret[l].astype(BF16), w_out[l].astype(BF16)
        wq = peer_w_q[l].astype(BF16)
        sk = peer_sub_keys[l]
        u_bf = peer_u[l].astype(BF16)
        vt_bf = peer_v[l].T.astype(BF16)

        zp = _inproj(xp, n1, w_in_bf, colw, cos_p, sin_p, tm_p, s // tm_p)
        zp3 = zp.reshape(b, s, IN_COLS)
        a_p = _attn_prompt(zp3, bias_p, b, s)
        for g, (window, d) in enumerate(ATTN_GROUPS):
            keep = min(window, s)
            k_rows = zp3[:, s - keep:, QA_W + g * gw:QA_W + (g + 1) * gw].reshape(b, keep, HEADS, HD_A)
            v_rows = zp3[:, s - keep:, 2 * QA_W + g * gw:2 * QA_W + (g + 1) * gw].reshape(b, keep, HEADS, HD_A)
            kv_p[g].append(jnp.stack([k_rows, v_rows], axis=2))
        ro, r_new = _ret_prompt(zp3, dec_p, xi_p, zeta_p, b, s)
        ret_p.append(r_new)
        x1, h2, pq = _merge(a_p, ro, zp, xp, woa, wor, wout, wq, bg, n2, 256 if (b * s) % 256 == 0 else ATTN_BLK)
        xp = _peer(h2, pq, x1, sk, u_bf, vt_bf)

        zs = _inproj(xs, n1, w_in_bf, colw, cos_s, sin_s, db, 1)
        zs3 = zs.reshape(db, 1, IN_COLS)
        qa_s, ka_s, va_s = zs3[:, :, :QA_W], zs3[:, :, QA_W:2 * QA_W], zs3[:, :, 2 * QA_W:3 * QA_W]
        c0 = 3 * QA_W
        qr_s, kr_s = zs3[:, :, c0:c0 + RQ_W], zs3[:, :, c0 + RQ_W:c0 + 2 * RQ_W]
        c1 = c0 + 2 * RQ_W
        vr_s, gr_s = zs3[:, :, c1:c1 + RV_W], zs3[:, :, c1 + RV_W:c1 + 2 * RV_W]
        a_s = _attn_sample(qa_s, ka_s, va_s, caches, l, brow_s, bself_s)
        for g in range(N_GROUPS):
            k_rows = ka_s[:, :, g * gw:(g + 1) * gw].reshape(db, 1, HEADS, HD_A)
            v_rows = va_s[:, :, g * gw:(g + 1) * gw].reshape(db, 1, HEADS, HD_A)
            kv_s[g].append(jnp.stack([k_rows, v_rows], axis=2))
        ro_s, r_s = _ret_sample(qr_s, kr_s, vr_s, gr_s, state_ret, l, gam_s)
        ret_s.append(r_s)
        x1s, h2s, pqs = _merge(a_s.reshape(db, gw), ro_s.reshape(db, RV_W), zs, xs, woa, wor, wout, wq, bg, n2, db)
        xs = _peer(h2s, pqs, x1s, sk, u_bf, vt_bf)

    return (xp.reshape(b, s, D_MODEL), xs.reshape(db, 1, D_MODEL),
            jnp.stack(kv_p[0]), jnp.stack(kv_p[1]), jnp.stack(kv_p[2]), jnp.stack(ret_p),
            jnp.stack(kv_s[0]), jnp.stack(kv_s[1]), jnp.stack(kv_s[2]), jnp.stack(ret_s))
```

```python
import functools

import numpy as np
import jax
import jax.numpy as jnp
from jax import lax
from jax.experimental import pallas as pl
from jax.experimental.pallas import tpu as pltpu

F32 = jnp.float32
BF16 = jnp.bfloat16

D_MODEL = 1024
PAST_LEN = 16384
ATTN_GROUPS = ((128, 1), (512, 4), (2048, 16))
N_GROUPS = 3
HEADS = 4
HD_A = 128
ATTN_BLK = 128
ATTN_ROWS = 2048
ATTN_TILES_PER_TRIP = 8
N_BUCKETS = 32
REL_MAX_DIST = 2048
HR = 4
DK_R = 128
DV_R = 256
RET_CHUNK = 128
N_KEYS = 128
N_EXPERTS = N_KEYS * N_KEYS
PEER_HEADS = 8
PEER_TOPK = 16
EPS = 1e-6
NEG = -1e30

QA_W = N_GROUPS * HEADS * HD_A
RQ_W = HR * DK_R
RV_W = HR * DV_R
IN_COLS = 3 * QA_W + 2 * RQ_W + 2 * RV_W + 2 * D_MODEL
COL_TILE = 512
N_COL_TILES = IN_COLS // COL_TILE
CT_K = QA_W // COL_TILE
CT_V = 2 * QA_W // COL_TILE
CT_QR = 3 * QA_W // COL_TILE
CT_KR = CT_QR + 1
CT_VR = CT_KR + 1
CT_GR = CT_VR + 2
CT_GATE = CT_GR + 2

LANES = 128
SUBLANES = 8
VMEM_LIMIT = 48 * 1024 * 1024

PEER_TOK_LANES = 128
PEER_EXPERT_BLK = 1024


def _dot(a, b, precision=None):
    return jnp.dot(a, b, preferred_element_type=F32, precision=precision)


def _dot_nt(a, b, precision=None):
    return lax.dot_general(a, b, (((1,), (1,)), ((), ())), preferred_element_type=F32, precision=precision)


def _cparams(sem):
    return pltpu.CompilerParams(dimension_semantics=sem, vmem_limit_bytes=VMEM_LIMIT)


def _inproj_kernel(x_ref, n1_ref, w_ref, colw_ref, cos_ref, sin_ref, z_ref, hn_ref):
    j = pl.program_id(1)

    @pl.when(j == 0)
    def _():
        x = x_ref[...]
        ms = jnp.mean(x * x, axis=-1, keepdims=True)
        hn_ref[...] = (x * lax.rsqrt(ms + EPS) * n1_ref[...]).astype(BF16)

    y = _dot(hn_ref[...], w_ref[...])
    cw = colw_ref[...]

    @pl.when(j < CT_V)
    def _():
        for h in range(HEADS):
            sl = slice(h * HD_A, (h + 1) * HD_A)
            yh = y[:, sl]
            ms = jnp.mean(yh * yh, axis=-1, keepdims=True)
            z_ref[:, sl] = yh * lax.rsqrt(ms + EPS) * cw[:, sl]

    @pl.when((j == CT_QR) | (j == CT_KR))
    def _():
        c = cos_ref[...]
        s = sin_ref[...]
        for h in range(HR):
            sl = slice(h * DK_R, (h + 1) * DK_R)
            yh = y[:, sl]
            z_ref[:, sl] = (yh * c + pltpu.roll(yh, DK_R // 2, 1) * s) * cw[:, sl]

    @pl.when(((j >= CT_V) & (j < CT_QR)) | (j >= CT_VR))
    def _():
        z_ref[...] = y


def _inproj(x, n1, w_bf, colw, cos_t, sin_t, tm, pos_blocks):
    t = x.shape[0]
    return pl.pallas_call(
        _inproj_kernel,
        grid=(t // tm, N_COL_TILES),
        in_specs=[
            pl.BlockSpec((tm, D_MODEL), lambda i, j: (i, 0)),
            pl.BlockSpec((1, D_MODEL), lambda i, j: (0, 0)),
            pl.BlockSpec((D_MODEL, COL_TILE), lambda i, j: (0, j)),
            pl.BlockSpec((None, 1, COL_TILE), lambda i, j: (j, 0, 0)),
            pl.BlockSpec((tm, DK_R), lambda i, j: (i % pos_blocks, 0)),
            pl.BlockSpec((tm, DK_R), lambda i, j: (i % pos_blocks, 0)),
        ],
        out_specs=pl.BlockSpec((tm, COL_TILE), lambda i, j: (i, j)),
        out_shape=jax.ShapeDtypeStruct((t, IN_COLS), F32),
        scratch_shapes=[pltpu.VMEM((tm, D_MODEL), BF16)],
        compiler_params=_cparams(("parallel", "arbitrary")),
        name="inproj",
    )(x, n1, w_bf, colw, cos_t, sin_t)


def _attn_scores(q, kp, kc, bias_p, bias_c, prev_ok):
    qi = lax.broadcasted_iota(jnp.int32, (ATTN_BLK, ATTN_BLK), 0)
    ki = lax.broadcasted_iota(jnp.int32, (ATTN_BLK, ATTN_BLK), 1)
    qb = q.astype(BF16)
    sp = _dot_nt(qb, kp.astype(BF16)) + bias_p
    sc = _dot_nt(qb, kc.astype(BF16)) + bias_c
    sp = jnp.where(jnp.logical_and(ki >= qi, prev_ok), sp, NEG)
    sc = jnp.where(ki <= qi, sc, NEG)
    return sp, sc


def _attn_softmax(sp, sc):
    m = jnp.maximum(jnp.max(sp, axis=-1, keepdims=True), jnp.max(sc, axis=-1, keepdims=True))
    pp = jnp.exp(sp - m)
    pc = jnp.exp(sc - m)
    den = jnp.sum(pp, axis=-1, keepdims=True) + jnp.sum(pc, axis=-1, keepdims=True)
    return pp.astype(BF16), pc.astype(BF16), den, m + jnp.log(den)


def _attn_prompt_kernel(*refs):
    q_refs, kc_refs, vc_refs, kp_refs, vp_refs = (refs[3 * n:3 * n + 3] for n in range(5))
    bias_ref, a_ref, o_scr, l_scr = refs[15:]
    i = pl.program_id(2)
    for g, (_, d) in enumerate(ATTN_GROUPS):
        span = ATTN_BLK * d
        nblk = ATTN_ROWS // span
        q_ref, kc_ref, vc_ref, kp_ref, vp_ref = q_refs[g], kc_refs[g], vc_refs[g], kp_refs[g], vp_refs[g]

        def tiles(it, carry, g=g, d=d, span=span, nblk=nblk,
                  q_ref=q_ref, kc_ref=kc_ref, vc_ref=vc_ref, kp_ref=kp_ref, vp_ref=vp_ref):
            where = []
            for u in range(ATTN_TILES_PER_TRIP):
                idx = it * ATTN_TILES_PER_TRIP + u
                r = idx % d
                c = idx // d
                rows = pl.ds(c * span + r, ATTN_BLK, stride=d)
                edge = pl.ds(r, ATTN_BLK, stride=d)
                inner = pl.ds(jnp.maximum(c - 1, 0) * span + r, ATTN_BLK, stride=d)
                where.append((c, rows, edge, inner))

            def prev_block(cur_ref, edge_ref, c, edge, inner):
                if nblk == 1:
                    return edge_ref[edge, :]
                return jnp.where(c == 0, edge_ref[edge, :], cur_ref[inner, :])

            scores = []
            for c, rows, edge, inner in where:
                prev_ok = jnp.logical_or(c > 0, i > 0)
                scores.append(_attn_scores(q_ref[rows, :], prev_block(kc_ref, kp_ref, c, edge, inner), kc_ref[rows, :],
                                           bias_ref[g, :, :ATTN_BLK], bias_ref[g, :, ATTN_BLK:], prev_ok))
            probs = [_attn_softmax(sp, sc) for sp, sc in scores]
            for (c, rows, edge, inner), (pp, pc, den, lse) in zip(where, probs):
                vp = prev_block(vc_ref, vp_ref, c, edge, inner)
                acc = _dot(pp, vp.astype(BF16)) + _dot(pc, vc_ref[rows, :].astype(BF16))
                o_scr[g, rows, :] = acc / den
                l_scr[g, rows, :] = jnp.broadcast_to(lse, (ATTN_BLK, HD_A))
            return carry

        lax.fori_loop(0, d * nblk // ATTN_TILES_PER_TRIP, tiles, 0)

    def combine(c, carry):
        rows = pl.ds(pl.multiple_of(c * ATTN_BLK, ATTN_BLK), ATTN_BLK)
        la, lb, lc = l_scr[0, rows, :], l_scr[1, rows, :], l_scr[2, rows, :]
        m = jnp.maximum(jnp.maximum(la, lb), lc)
        ea, eb, ec = jnp.exp(la - m), jnp.exp(lb - m), jnp.exp(lc - m)
        a_ref[rows, :] = (ea * o_scr[0, rows, :] + eb * o_scr[1, rows, :] + ec * o_scr[2, rows, :]) / (ea + eb + ec)
        return carry

    lax.fori_loop(0, ATTN_ROWS // ATTN_BLK, combine, 0)


def _attn_prompt(z3, bias, b, s):
    col = lambda base, g: (lambda bb, h, i, c0=(base + g * HEADS * HD_A) // HD_A: (bb, i, c0 + h))

    def prev_spec(base, g, d):
        span = ATTN_BLK * d
        nblk = ATTN_ROWS // span
        c0 = (base + g * HEADS * HD_A) // HD_A
        return pl.BlockSpec((None, span, HD_A), lambda bb, h, i: (bb, jnp.maximum(i * nblk - 1, 0), c0 + h))

    cur = lambda base, g: pl.BlockSpec((None, ATTN_ROWS, HD_A), col(base, g))
    groups = range(N_GROUPS)
    in_specs = ([cur(0, g) for g in groups] + [cur(QA_W, g) for g in groups] + [cur(2 * QA_W, g) for g in groups]
                + [prev_spec(QA_W, g, d) for g, (_, d) in enumerate(ATTN_GROUPS)]
                + [prev_spec(2 * QA_W, g, d) for g, (_, d) in enumerate(ATTN_GROUPS)]
                + [pl.BlockSpec((N_GROUPS, None, ATTN_BLK, 2 * ATTN_BLK), lambda bb, h, i: (0, h, 0, 0))])
    a = pl.pallas_call(
        _attn_prompt_kernel,
        grid=(b, HEADS, s // ATTN_ROWS),
        in_specs=in_specs,
        out_specs=pl.BlockSpec((None, ATTN_ROWS, HD_A), lambda bb, h, i: (bb, i, h)),
        out_shape=jax.ShapeDtypeStruct((b, s, HEADS * HD_A), F32),
        scratch_shapes=[pltpu.VMEM((N_GROUPS, ATTN_ROWS, HD_A), F32), pltpu.VMEM((N_GROUPS, ATTN_ROWS, HD_A), F32)],
        compiler_params=_cparams(("parallel", "parallel", "arbitrary")),
        name="attn_prompt",
    )(*([z3] * 15), bias)
    return a.reshape(b * s, HEADS * HD_A)


def _attn_sample_kernel(q_ref, k_ref, v_ref, c0_ref, c1_ref, c2_ref, brow_ref, bself_ref, a_ref):
    caches = (c0_ref, c1_ref, c2_ref)
    gw = HEADS * HD_A
    for h in range(HEADS):
        outs, lses = [], []
        for g in range(N_GROUPS):
            col = g * gw + h * HD_A
            q = q_ref[:, col:col + HD_A]
            kn = k_ref[:, col:col + HD_A]
            vn = v_ref[:, col:col + HD_A]
            kc = caches[g][:, 0, h, :]
            vc = caches[g][:, 1, h, :]
            sc = jnp.sum(kc * q, axis=-1, keepdims=True) + brow_ref[g, :, h:h + 1]
            s0 = jnp.sum(kn * q, axis=-1, keepdims=True) + bself_ref[g, :, h:h + 1][0:1]
            m = jnp.maximum(jnp.max(sc, axis=0, keepdims=True), s0)
            e = jnp.exp(sc - m)
            e0 = jnp.exp(s0 - m)
            den = jnp.sum(e, axis=0, keepdims=True) + e0
            outs.append((jnp.sum(e * vc, axis=0, keepdims=True) + e0 * vn) / den)
            lses.append(m + jnp.log(den))
        mm = jnp.maximum(jnp.maximum(lses[0], lses[1]), lses[2])
        ws = [jnp.exp(l - mm) for l in lses]
        wsum = ws[0] + ws[1] + ws[2]
        a_ref[:, h * HD_A:(h + 1) * HD_A] = (ws[0] * outs[0] + ws[1] * outs[1] + ws[2] * outs[2]) / wsum


def _attn_sample(qa, ka, va, caches, layer, brow, bself):
    db = qa.shape[0]
    gw = HEADS * HD_A
    cviews, cspecs = [], []
    for c, (window, d) in zip(caches, ATTN_GROUPS):
        assert c.shape[2] == window, "sample attention expects a full window of cached rows"
        cviews.append(c.reshape(c.shape[0], db, window // d, d, 2, HEADS, HD_A))
        cspecs.append(pl.BlockSpec((None, None, ATTN_BLK, None, 2, HEADS, HD_A), lambda b: (layer, b, 0, 0, 0, 0, 0)))
    tok = pl.BlockSpec((None, 1, QA_W), lambda b: (b, 0, 0))
    return pl.pallas_call(
        _attn_sample_kernel,
        grid=(db,),
        in_specs=[tok, tok, tok] + cspecs + [
            pl.BlockSpec((N_GROUPS, ATTN_BLK, HEADS), lambda b: (0, 0, 0)),
            pl.BlockSpec((N_GROUPS, 8, HEADS), lambda b: (0, 0, 0)),
        ],
        out_specs=pl.BlockSpec((None, 1, gw), lambda b: (b, 0, 0)),
        out_shape=jax.ShapeDtypeStruct((db, 1, gw), F32),
        compiler_params=_cparams(("parallel",)),
        name="attn_sample",
    )(qa, ka, va, *cviews, brow, bself)


def _ret_epilogue(o, gate):
    on = o * lax.rsqrt(jnp.mean(o * o, axis=-1, keepdims=True) + EPS)
    return on * (gate / (1.0 + jnp.exp(-gate)))


def _ret_prompt_kernel(q_ref, k_ref, va_ref, vb_ref, ga_ref, gb_ref, dec_ref, xi_ref, zeta_ref, o_ref, rout_ref, r_scr):
    c = pl.program_id(0)
    heads_per_tile = COL_TILE // DV_R
    nb = q_ref.shape[0]

    @pl.when(c == 0)
    def _():
        r_scr[...] = jnp.zeros_like(r_scr)

    pending = []
    for bb in range(nb):
        for h in range(HR):
            qs = slice(h * DK_R, (h + 1) * DK_R)
            ts = slice((h % heads_per_tile) * DV_R, (h % heads_per_tile + 1) * DV_R)
            v_ref, g_ref = ((va_ref, ga_ref), (vb_ref, gb_ref))[h // heads_per_tile]
            k = k_ref[bb, :, qs]
            qb = q_ref[bb, :, qs].astype(BF16)
            vb = v_ref[bb, :, ts].astype(BF16)
            r = r_scr[bb, h]
            xi = xi_ref[h]
            s = _dot_nt(qb, k.astype(BF16)) * dec_ref[h]
            cross = _dot(qb, r.astype(BF16)) * xi
            kz = (k * zeta_ref[h]).T
            r_scr[bb, h] = xi[RET_CHUNK - 1:RET_CHUNK, :] * r + _dot(kz.astype(BF16), vb)
            pending.append((bb, h, ts, g_ref, s.astype(BF16), vb, cross))
    for bb, h, ts, g_ref, sb, vb, cross in pending:
        o = _dot(sb, vb) + cross
        o_ref[bb, :, h * DV_R:(h + 1) * DV_R] = _ret_epilogue(o, g_ref[bb, :, ts])

    @pl.when(c == pl.num_programs(0) - 1)
    def _():
        rout_ref[...] = r_scr[...]


def _ret_prompt(z3, dec, xi, zeta, b, s):
    nc = s // RET_CHUNK
    qk_w = HR * DK_R
    tab = lambda w: pl.BlockSpec((HR, RET_CHUNK, w), lambda c: (0, 0, 0))
    col = lambda w, tile: pl.BlockSpec((b, RET_CHUNK, w), lambda c: (0, c, tile))
    o, r = pl.pallas_call(
        _ret_prompt_kernel,
        grid=(nc,),
        in_specs=[
            col(qk_w, CT_QR), col(qk_w, CT_KR),
            col(COL_TILE, CT_VR), col(COL_TILE, CT_VR + 1), col(COL_TILE, CT_GR), col(COL_TILE, CT_GR + 1),
            tab(RET_CHUNK), tab(DV_R), tab(DK_R),
        ],
        out_specs=[
            pl.BlockSpec((b, RET_CHUNK, RV_W), lambda c: (0, c, 0)),
            pl.BlockSpec((b, HR, DK_R, DV_R), lambda c: (0, 0, 0, 0)),
        ],
        out_shape=[jax.ShapeDtypeStruct((b, s, RV_W), F32), jax.ShapeDtypeStruct((b, HR, DK_R, DV_R), F32)],
        scratch_shapes=[pltpu.VMEM((b, HR, DK_R, DV_R), F32)],
        compiler_params=_cparams(("arbitrary",)),
        name="ret_prompt",
    )(z3, z3, z3, z3, z3, z3, dec, xi, zeta)
    return o.reshape(b * s, RV_W), r


def _ret_sample_kernel(q_ref, k_ref, v_ref, g_ref, r0_ref, gam_ref, o_ref, rout_ref):
    eye = (lax.broadcasted_iota(jnp.int32, (DK_R, DK_R), 0) == lax.broadcasted_iota(jnp.int32, (DK_R, DK_R), 1))
    for h in range(HR):
        qs = slice(h * DK_R, (h + 1) * DK_R)
        vs = slice(h * DV_R, (h + 1) * DV_R)
        q = q_ref[:, qs]
        k = k_ref[:, qs]
        v = v_ref[:, vs]
        r = r0_ref[h]
        gam = gam_ref[h]
        qcol = jnp.sum(jnp.where(eye, jnp.broadcast_to(q, (DK_R, DK_R)), 0.0), axis=-1, keepdims=True)
        kcol = jnp.sum(jnp.where(eye, jnp.broadcast_to(k, (DK_R, DK_R)), 0.0), axis=-1, keepdims=True)
        qk = jnp.sum(q * k, axis=-1, keepdims=True)
        o = qk * v + jnp.sum(qcol * r, axis=0, keepdims=True) * gam
        rout_ref[h] = gam * r + kcol * v
        o_ref[:, vs] = _ret_epilogue(o, g_ref[:, vs])


def _ret_sample(qr, kr, vr, gr, state, layer, gam):
    db = qr.shape[0]
    qk_w = HR * DK_R
    return pl.pallas_call(
        _ret_sample_kernel,
        grid=(db,),
        in_specs=[
            pl.BlockSpec((None, 1, qk_w), lambda b: (b, 0, 0)),
            pl.BlockSpec((None, 1, qk_w), lambda b: (b, 0, 0)),
            pl.BlockSpec((None, 1, RV_W), lambda b: (b, 0, 0)),
            pl.BlockSpec((None, 1, RV_W), lambda b: (b, 0, 0)),
            pl.BlockSpec((None, None, HR, DK_R, DV_R), lambda b: (layer, b, 0, 0, 0)),
            pl.BlockSpec((HR, 1, DV_R), lambda b: (0, 0, 0)),
        ],
        out_specs=[
            pl.BlockSpec((None, 1, RV_W), lambda b: (b, 0, 0)),
            pl.BlockSpec((None, HR, DK_R, DV_R), lambda b: (b, 0, 0, 0)),
        ],
        out_shape=[jax.ShapeDtypeStruct((db, 1, RV_W), F32), jax.ShapeDtypeStruct((db, HR, DK_R, DV_R), F32)],
        compiler_params=_cparams(("parallel",)),
        name="ret_sample",
    )(qr, kr, vr, gr, state, gam)


def _merge_kernel(a_ref, ro_ref, ga0, ga1, gb0, gb1, x_ref, woa_ref, wor_ref, wout_ref, wq_ref, bg_ref, n2_ref,
                  x1_ref, h2_ref, pq_ref):
    a = a_ref[...]
    ba = _dot(a.astype(BF16), woa_ref[...])
    br = _dot(ro_ref[...].astype(BF16), wor_ref[...])
    ga = jnp.concatenate([ga0[...], ga1[...]], axis=-1) + bg_ref[:, :D_MODEL]
    gb = jnp.concatenate([gb0[...], gb1[...]], axis=-1) + bg_ref[:, D_MODEL:]
    mixed = ba / (1.0 + jnp.exp(-ga)) + br / (1.0 + jnp.exp(-gb))
    x1 = x_ref[...] + _dot(mixed.astype(BF16), wout_ref[...])
    x1_ref[...] = x1
    h2 = (x1 * lax.rsqrt(jnp.mean(x1 * x1, axis=-1, keepdims=True) + EPS) * n2_ref[...]).astype(BF16)
    h2_ref[...] = h2
    pq_ref[...] = _dot(h2, wq_ref[...])


def _merge(a, ro, z, x, woa, wor, wout, wq, bg, n2, tm):
    t = x.shape[0]
    gw = HEADS * HD_A
    row = lambda w, c=0: pl.BlockSpec((tm, w), lambda i, c=c: (i, c))
    full = lambda a: pl.BlockSpec(a.shape, lambda i: (0, 0))
    in_specs = [
        row(gw), row(RV_W),
        row(COL_TILE, CT_GATE), row(COL_TILE, CT_GATE + 1), row(COL_TILE, CT_GATE + 2), row(COL_TILE, CT_GATE + 3),
        row(D_MODEL),
        full(woa), full(wor), full(wout), full(wq), full(bg), full(n2),
    ]
    return pl.pallas_call(
        _merge_kernel,
        grid=(t // tm,),
        in_specs=in_specs,
        out_specs=[row(D_MODEL), row(D_MODEL), row(wq.shape[1])],
        out_shape=[jax.ShapeDtypeStruct((t, D_MODEL), F32), jax.ShapeDtypeStruct((t, D_MODEL), BF16),
                   jax.ShapeDtypeStruct((t, wq.shape[1]), F32)],
        compiler_params=_cparams(("parallel",)),
        name="merge",
    )(a, ro, z, z, z, z, x, woa, wor, wout, wq, bg, n2)


def _topk_rank(s, break_ties):
    key = lax.broadcasted_iota(jnp.int32, s.shape, 0).astype(F32)
    rank = jnp.full(s.shape, float(PEER_TOPK), F32)
    vals = []
    for k in range(PEER_TOPK):
        m = jnp.max(s, axis=0, keepdims=True)
        hit = s == m
        if break_ties:
            hit = key == jnp.min(jnp.where(hit, key, float(N_KEYS)), axis=0, keepdims=True)
        rank = jnp.where(hit, float(k), rank)
        s = jnp.where(hit, -jnp.inf, s)
        vals.append(m)
    return rank, vals


def _exactly_16(picked):
    return jnp.where(jnp.sum(picked, axis=0, keepdims=True) != float(PEER_TOPK), 1.0, 0.0)


def _peer_select_head(pq_ref, sk_ref, row_ref, tile_ref, hd, break_ties):
    ntok = pq_ref.shape[0]
    hi = lax.Precision.HIGHEST
    sub = lax.broadcasted_iota(jnp.int32, (SUBLANES, ntok), 0).astype(F32)
    flat = jnp.concatenate([sub, 8.0 + sub, 16.0 + sub, 32.0 + sub, 48.0 + sub,
                            (8.0 + sub) * 16.0, sub * 16.0, sub * 16.0 + 1.0, sub * 16.0 + 2.0], axis=0)
    ninf = jnp.full((SUBLANES, ntok), -jnp.inf, F32)
    sc = []
    for p in range(2):
        col = (hd * 2 + p) * N_KEYS
        sc.append(_dot_nt(sk_ref[p], pq_ref[:, col:col + N_KEYS], hi))
    rank1, sv1 = _topk_rank(sc[0], break_ties)
    rank2, sv2 = _topk_rank(sc[1], break_ties)
    a_lo, a_hi = jnp.concatenate(sv1[:8], axis=0), jnp.concatenate(sv1[8:], axis=0)
    b_lo, b_hi = jnp.concatenate(sv2[:8], axis=0), jnp.concatenate(sv2[8:], axis=0)
    cand = jnp.concatenate([
        sv1[0] + b_lo, sv1[0] + b_hi, sv1[1] + b_lo,
        jnp.where(sub < 5.0, sv1[2] + b_lo, ninf),
        jnp.where(sub < 4.0, sv1[3] + b_lo, ninf),
        a_hi + sv2[0],
        jnp.where(sub >= 4.0, a_lo + sv2[0], ninf),
        jnp.where(sub >= 4.0, a_lo + sv2[1], ninf),
        jnp.where(sub == 4.0, a_lo + sv2[2], ninf),
    ], axis=0)
    work = cand
    chosen = jnp.zeros(cand.shape, F32)
    for _ in range(PEER_TOPK):
        m = jnp.max(work, axis=0, keepdims=True)
        hit = work == m
        if break_ties:
            hit = flat == jnp.min(jnp.where(hit, flat, float(PEER_TOPK * PEER_TOPK)), axis=0, keepdims=True)
        chosen = jnp.where(hit, 1.0, chosen)
        work = jnp.where(hit, -jnp.inf, work)
    top = sv1[0] + sv2[0]
    zsum = jnp.sum(chosen * jnp.exp(cand - top), axis=0, keepdims=True)
    cnt_rows = [jnp.sum(chosen[0:16], axis=0, keepdims=True)]
    cnt_rows += [jnp.sum(chosen[8 * g:8 * g + 8], axis=0, keepdims=True) for g in (2, 3, 4)]
    cnt_mid = chosen[48:56] + chosen[56:64] + chosen[64:72]
    cnt_rows += [cnt_mid[a:a + 1] for a in range(4, 8)]
    cnt_rows += [chosen[40 + a:41 + a] for a in range(8)]
    cnt1 = jnp.zeros((N_KEYS, ntok), F32)
    for a in range(PEER_TOPK):
        cnt1 = jnp.where(rank1 == float(a), cnt_rows[a], cnt1)
    row_ref[hd, 0] = cnt1
    row_ref[hd, 1] = jnp.exp(sc[0] - sv1[0]) / zsum
    tile_ref[hd, 0] = rank2
    tile_ref[hd, 1] = jnp.exp(sc[1] - sv2[0])
    if break_ties:
        return None
    in_top = lambda rank: jnp.where(rank < float(PEER_TOPK), 1.0, 0.0)
    return _exactly_16(in_top(rank1)) + _exactly_16(in_top(rank2)) + _exactly_16(chosen)


def _peer_select_kernel(pq_ref, sk_ref, row_ref, tile_ref):
    ties = None
    for hd in range(PEER_HEADS):
        t = _peer_select_head(pq_ref, sk_ref, row_ref, tile_ref, hd, break_ties=False)
        ties = t if ties is None else ties + t

    @pl.when(jnp.max(ties) > 0.0)
    def _():
        for hd in range(PEER_HEADS):
            _peer_select_head(pq_ref, sk_ref, row_ref, tile_ref, hd, break_ties=True)


def _peer_select(pq, sk):
    t = pq.shape[0]
    tl = PEER_TOK_LANES
    shape = (t // tl, PEER_HEADS, 2, N_KEYS, tl)
    spec = pl.BlockSpec((None, PEER_HEADS, 2, N_KEYS, tl), lambda i: (i, 0, 0, 0, 0))
    return pl.pallas_call(
        _peer_select_kernel,
        grid=(t // tl,),
        in_specs=[
            pl.BlockSpec((tl, pq.shape[1]), lambda i: (i, 0)),
            pl.BlockSpec(sk.shape, lambda i: (0, 0, 0)),
        ],
        out_specs=[spec, spec],
        out_shape=[jax.ShapeDtypeStruct(shape, F32), jax.ShapeDtypeStruct(shape, F32)],
        compiler_params=_cparams(("parallel",)),
        name="peer_select",
    )(pq, sk)


def _gelu(x):
    return 0.5 * x * (1.0 + lax.erf(x * np.float32(2.0 ** -0.5)))


def _peer_dense_kernel(h_ref, row_ref, tile_ref, u_ref, vt_ref, x_ref, out_ref, act_ref, g_ref, acc_ref, *, nsub):
    e = pl.program_id(1)
    keys_per_blk = PEER_EXPERT_BLK // N_KEYS
    groups = N_KEYS // SUBLANES
    vreg = (SUBLANES, PEER_TOK_LANES)

    @pl.when(e == 0)
    def _():
        acc_ref[...] = jnp.zeros_like(acc_ref)

    act_ref[...] = _gelu(_dot_nt(u_ref[...], h_ref[...]))

    def gate(kk, carry):
        i = e * keys_per_blk + kk
        row0 = pl.multiple_of(kk * N_KEYS, N_KEYS)
        for t in range(nsub):
            lanes = slice(t * PEER_TOK_LANES, (t + 1) * PEER_TOK_LANES)
            w = [None] * groups
            for hd in range(PEER_HEADS):
                cnt = jnp.broadcast_to(row_ref[t, hd, 0, pl.ds(i, 1), :], vreg)
                e1 = jnp.broadcast_to(row_ref[t, hd, 1, pl.ds(i, 1), :], vreg)
                for g in range(groups):
                    keys = slice(g * SUBLANES, (g + 1) * SUBLANES)
                    term = jnp.where(tile_ref[t, hd, 0, keys, :] < cnt, e1 * tile_ref[t, hd, 1, keys, :], 0.0)
                    w[g] = term if w[g] is None else w[g] + term
            rows = pl.ds(row0, N_KEYS)
            g_ref[rows, lanes] = (jnp.concatenate(w, axis=0) * act_ref[rows, lanes]).astype(BF16)
        return carry

    lax.fori_loop(0, keys_per_blk, gate, 0)
    acc_ref[...] += _dot(vt_ref[...], g_ref[...])

    @pl.when(e == pl.num_programs(1) - 1)
    def _():
        out_ref[...] = x_ref[...] + acc_ref[...].T


def _peer_dense(h2, sel_rows, sel_tiles, u_bf, vt_bf, x1, tb):
    t = h2.shape[0]
    nsub = tb // PEER_TOK_LANES
    eb = PEER_EXPERT_BLK
    sel_spec = pl.BlockSpec((nsub, PEER_HEADS, 2, N_KEYS, PEER_TOK_LANES), lambda i, e: (i, 0, 0, 0, 0))
    return pl.pallas_call(
        functools.partial(_peer_dense_kernel, nsub=nsub),
        grid=(t // tb, N_EXPERTS // eb),
        in_specs=[
            pl.BlockSpec((tb, D_MODEL), lambda i, e: (i, 0)),
            sel_spec,
            sel_spec,
            pl.BlockSpec((eb, D_MODEL), lambda i, e: (e, 0)),
            pl.BlockSpec((D_MODEL, eb), lambda i, e: (0, e)),
            pl.BlockSpec((tb, D_MODEL), lambda i, e: (i, 0)),
        ],
        out_specs=pl.BlockSpec((tb, D_MODEL), lambda i, e: (i, 0)),
        out_shape=jax.ShapeDtypeStruct((t, D_MODEL), F32),
        scratch_shapes=[
            pltpu.VMEM((eb, tb), F32),
            pltpu.VMEM((eb, tb), BF16),
            pltpu.VMEM((D_MODEL, tb), F32),
        ],
        compiler_params=_cparams(("parallel", "arbitrary")),
        name="peer_dense",
    )(h2, sel_rows, sel_tiles, u_bf, vt_bf, x1)


def _peer(h2, pq, x1, sk, u_bf, vt_bf):
    t = h2.shape[0]
    tb = min(512, -(-t // PEER_TOK_LANES) * PEER_TOK_LANES)
    t_pad = -(-t // tb) * tb
    if t_pad != t:
        h2 = jnp.pad(h2, ((0, t_pad - t), (0, 0)))
        x1 = jnp.pad(x1, ((0, t_pad - t), (0, 0)))
        pq = jnp.pad(pq, ((0, t_pad - t), (0, 0)))
    sel_rows, sel_tiles = _peer_select(pq, sk)
    return _peer_dense(h2, sel_rows, sel_tiles, u_bf, vt_bf, x1, tb)[:t]


def _t5_bucket(dist):
    dist = np.asarray(dist, np.int64)
    max_exact = N_BUCKETS // 2
    ratio = np.log(np.maximum(dist, 1) / max_exact) / np.log(REL_MAX_DIST / max_exact)
    large = np.minimum(max_exact + (ratio * (N_BUCKETS - max_exact)).astype(np.int64), N_BUCKETS - 1)
    return np.where(dist < max_exact, dist, large).astype(np.int32)


def _rotary_tables(pos):
    half = DK_R // 2
    ang = jnp.asarray(1.0 / (10000.0 ** np.linspace(0.0, 1.0, half)), F32)
    theta = pos.astype(F32)[:, None] * ang[None, :]
    c, s = jnp.cos(theta), jnp.sin(theta)
    return jnp.concatenate([c, c], axis=-1), jnp.concatenate([-s, s], axis=-1)


def _retention_tables(length, log_g):
    n = np.arange(length)
    diff = n[:, None] - n[None, :]
    decay = jnp.where(jnp.asarray(diff >= 0)[None],
                      jnp.exp(jnp.asarray(np.maximum(diff, 0), F32)[None] * log_g[:, None, None]), 0.0)
    xi = jnp.exp(jnp.asarray(n + 1, F32)[None, :] * log_g[:, None])
    zeta = jnp.exp(jnp.asarray(n[::-1].copy(), F32)[None, :] * log_g[:, None])
    xi = jnp.broadcast_to(xi[:, :, None], (HR, length, DV_R))
    zeta = jnp.broadcast_to(zeta[:, :, None], (HR, length, DK_R))
    return decay, xi, zeta


def _prompt_bias(bias_g, g, d):
    steps = ATTN_GROUPS[g][0] // d
    assert steps == ATTN_BLK
    per_dist = bias_g[:, g][_t5_bucket(np.arange(steps + 1) * d)]
    n = 3 * ATTN_BLK
    diag = per_dist[np.clip(2 * ATTN_BLK - 1 - np.arange(n), 0, steps)].T
    rows = jnp.tile(diag, (1, ATTN_BLK + 1))[:, :ATTN_BLK * (n + 1)].reshape(HEADS, ATTN_BLK, n + 1)
    return rows[:, ::-1, :2 * ATTN_BLK]


def _sample_bias(bias_g):
    rows, selfs = [], []
    for g, (window, d) in enumerate(ATTN_GROUPS):
        steps = window // d
        assert steps == ATTN_BLK
        j = steps - np.arange(steps)
        rows.append(bias_g[:, g][_t5_bucket(j * d)])
        selfs.append(jnp.broadcast_to(bias_g[:, g][_t5_bucket(np.zeros(1))], (8, HEADS)))
    return jnp.stack(rows), jnp.stack(selfs)


def kernel(x_prompt, x_sample, cache_kv_d1, cache_kv_d4, cache_kv_d16, state_ret, rel_bias, norm1_w, w_in, b_gate, q_norm_w, k_norm_w, w_o_attn, w_o_ret, w_out, norm2_w, peer_w_q, peer_sub_keys, peer_u, peer_v):
    b, s, _ = x_prompt.shape
    db, ds, _ = x_sample.shape
    assert ds == 1, "the sample group is one new token per sequence"
    depth = w_in.shape[0]
    gw = HEADS * HD_A
    caches = (cache_kv_d1, cache_kv_d4, cache_kv_d16)

    bias_g = rel_bias.reshape(N_BUCKETS, N_GROUPS, HEADS).astype(F32)
    log_g = jnp.asarray(np.log(1.0 - 2.0 ** (-5.0 - np.arange(HR))), F32)
    cos_p, sin_p = _rotary_tables(jnp.arange(s, dtype=jnp.int32))
    cos_s, sin_s = _rotary_tables(jnp.full((db,), PAST_LEN, jnp.int32))
    dec_p, xi_p, zeta_p = _retention_tables(RET_CHUNK, log_g)
    gam_s = _retention_tables(1, log_g)[1]
    bias_p = jnp.stack([_prompt_bias(bias_g, g, d) for g, (_, d) in enumerate(ATTN_GROUPS)])
    brow_s, bself_s = _sample_bias(bias_g)

    tm_p = next(t for t in (1024, 512, ATTN_BLK) if s % t == 0)
    xp = x_prompt.reshape(b * s, D_MODEL)
    xs = x_sample.reshape(db, D_MODEL)
    kv_p = [[] for _ in range(N_GROUPS)]
    kv_s = [[] for _ in range(N_GROUPS)]
    ret_p, ret_s = [], []

    for l in range(depth):
        w_in_bf = w_in[l].astype(BF16)
        ones = jnp.ones((COL_TILE,), F32)
        colw = jnp.stack(
            [jnp.tile(q_norm_w[l, g], HEADS) * (HD_A ** -0.5) for g in range(N_GROUPS)]
            + [jnp.tile(k_norm_w[l, g], HEADS) for g in range(N_GROUPS)]
            + [ones] * (CT_KR - CT_V) + [ones * (DK_R ** -0.5)] + [ones] * (N_COL_TILES - CT_KR - 1)
        ).reshape(N_COL_TILES, 1, COL_TILE)
        n1 = norm1_w[l].reshape(1, D_MODEL)
        n2 = norm2_w[l].reshape(1, D_MODEL)
        bg = b_gate[l].reshape(1, 2 * D_MODEL)
        woa, wor, wout = w_o_attn[l].astype(BF16), w_o_ret[l].astype(BF16), w_out[l].astype(BF16)
        wq = peer_w_q[l].astype(BF16)
        sk = peer_sub_keys[l]
        u_bf = peer_u[l].astype(BF16)
        vt_bf = peer_v[l].T.astype(BF16)

        zp = _inproj(xp, n1, w_in_bf, colw, cos_p, sin_p, tm_p, s // tm_p)
        zp3 = zp.reshape(b, s, IN_COLS)
        a_p = _attn_prompt(zp3, bias_p, b, s)
        for g, (window, d) in enumerate(ATTN_GROUPS):
            keep = min(window, s)
            k_rows = zp3[:, s - keep:, QA_W + g * gw:QA_W + (g + 1) * gw].reshape(b, keep, HEADS, HD_A)
            v_rows = zp3[:, s - keep:, 2 * QA_W + g * gw:2 * QA_W + (g + 1) * gw].reshape(b, keep, HEADS, HD_A)
            kv_p[g].append(jnp.stack([k_rows, v_rows], axis=2))
        ro, r_new = _ret_prompt(zp3, dec_p, xi_p, zeta_p, b, s)
        ret_p.append(r_new)
        x1, h2, pq = _merge(a_p, ro, zp, xp, woa, wor, wout, wq, bg, n2, 256 if (b * s) % 256 == 0 else ATTN_BLK)
        xp = _peer(h2, pq, x1, sk, u_bf, vt_bf)

        zs = _inproj(xs, n1, w_in_bf, colw, cos_s, sin_s, db, 1)
        zs3 = zs.reshape(db, 1, IN_COLS)
        qa_s, ka_s, va_s = zs3[:, :, :QA_W], zs3[:, :, QA_W:2 * QA_W], zs3[:, :, 2 * QA_W:3 * QA_W]
        c0 = 3 * QA_W
        qr_s, kr_s = zs3[:, :, c0:c0 + RQ_W], zs3[:, :, c0 + RQ_W:c0 + 2 * RQ_W]
        c1 = c0 + 2 * RQ_W
        vr_s, gr_s = zs3[:, :, c1:c1 + RV_W], zs3[:, :, c1 + RV_W:c1 + 2 * RV_W]
        a_s = _attn_sample(qa_s, ka_s, va_s, caches, l, brow_s, bself_s)
        for g in range(N_GROUPS):
            k_rows = ka_s[:, :, g * gw:(g + 1) * gw].reshape(db, 1, HEADS, HD_A)
            v_rows = va_s[:, :, g * gw:(g + 1) * gw].reshape(db, 1, HEADS, HD_A)
            kv_s[g].append(jnp.stack([k_rows, v_rows], axis=2))
        ro_s, r_s = _ret_sample(qr_s, kr_s, vr_s, gr_s, state_ret, l, gam_s)
        ret_s.append(r_s)
        x1s, h2s, pqs = _merge(a_s.reshape(db, gw), ro_s.reshape(db, RV_W), zs, xs, woa, wor, wout, wq, bg, n2, db)
        xs = _peer(h2s, pqs, x1s, sk, u_bf, vt_bf)

    return (xp.reshape(b, s, D_MODEL), xs.reshape(db, 1, D_MODEL),
            jnp.stack(kv_p[0]), jnp.stack(kv_p[1]), jnp.stack(kv_p[2]), jnp.stack(ret_p),
            jnp.stack(kv_s[0]), jnp.stack(kv_s[1]), jnp.stack(kv_s[2]), jnp.stack(ret_s))
```

```python
import functools

import numpy as np
import jax
import jax.numpy as jnp
from jax import lax
from jax.experimental import pallas as pl
from jax.experimental.pallas import tpu as pltpu

F32 = jnp.float32
BF16 = jnp.bfloat16

D_MODEL = 1024
PAST_LEN = 16384
ATTN_GROUPS = ((128, 1), (512, 4), (2048, 16))
N_GROUPS = 3
HEADS = 4
HD_A = 128
ATTN_BLK = 128
ATTN_ROWS = 2048
ATTN_TILES_PER_TRIP = 8
N_BUCKETS = 32
REL_MAX_DIST = 2048
HR = 4
DK_R = 128
DV_R = 256
RET_CHUNK = 128
N_KEYS = 128
N_EXPERTS = N_KEYS * N_KEYS
PEER_HEADS = 8
PEER_TOPK = 16
EPS = 1e-6
NEG = -1e30

QA_W = N_GROUPS * HEADS * HD_A
RQ_W = HR * DK_R
RV_W = HR * DV_R
IN_COLS = 3 * QA_W + 2 * RQ_W + 2 * RV_W + 2 * D_MODEL
COL_TILE = 512
N_COL_TILES = IN_COLS // COL_TILE
CT_K = QA_W // COL_TILE
CT_V = 2 * QA_W // COL_TILE
CT_QR = 3 * QA_W // COL_TILE
CT_KR = CT_QR + 1
CT_VR = CT_KR + 1
CT_GR = CT_VR + 2
CT_GATE = CT_GR + 2

LANES = 128
SUBLANES = 8
VMEM_LIMIT = 48 * 1024 * 1024

PEER_TOK_LANES = 128
PEER_EXPERT_BLK = 1024


def _dot(a, b, precision=None):
    return jnp.dot(a, b, preferred_element_type=F32, precision=precision)


def _dot_nt(a, b, precision=None):
    return lax.dot_general(a, b, (((1,), (1,)), ((), ())), preferred_element_type=F32, precision=precision)


def _cparams(sem):
    return pltpu.CompilerParams(dimension_semantics=sem, vmem_limit_bytes=VMEM_LIMIT)


def _inproj_kernel(x_ref, n1_ref, w_ref, colw_ref, cos_ref, sin_ref, z_ref, hn_ref):
    j = pl.program_id(1)

    @pl.when(j == 0)
    def _():
        x = x_ref[...]
        ms = jnp.mean(x * x, axis=-1, keepdims=True)
        hn_ref[...] = (x * lax.rsqrt(ms + EPS) * n1_ref[...]).astype(BF16)

    y = _dot(hn_ref[...], w_ref[...])
    cw = colw_ref[...]

    @pl.when(j < CT_V)
    def _():
        for h in range(HEADS):
            sl = slice(h * HD_A, (h + 1) * HD_A)
            yh = y[:, sl]
            ms = jnp.mean(yh * yh, axis=-1, keepdims=True)
            z_ref[:, sl] = yh * lax.rsqrt(ms + EPS) * cw[:, sl]

    @pl.when((j == CT_QR) | (j == CT_KR))
    def _():
        c = cos_ref[...]
        s = sin_ref[...]
        for h in range(HR):
            sl = slice(h * DK_R, (h + 1) * DK_R)
            yh = y[:, sl]
            z_ref[:, sl] = (yh * c + pltpu.roll(yh, DK_R // 2, 1) * s) * cw[:, sl]

    @pl.when(((j >= CT_V) & (j < CT_QR)) | (j >= CT_VR))
    def _():
        z_ref[...] = y


def _inproj(x, n1, w_bf, colw, cos_t, sin_t, tm, pos_blocks):
    t = x.shape[0]
    return pl.pallas_call(
        _inproj_kernel,
        grid=(t // tm, N_COL_TILES),
        in_specs=[
            pl.BlockSpec((tm, D_MODEL), lambda i, j: (i, 0)),
            pl.BlockSpec((1, D_MODEL), lambda i, j: (0, 0)),
            pl.BlockSpec((D_MODEL, COL_TILE), lambda i, j: (0, j)),
            pl.BlockSpec((None, 1, COL_TILE), lambda i, j: (j, 0, 0)),
            pl.BlockSpec((tm, DK_R), lambda i, j: (i % pos_blocks, 0)),
            pl.BlockSpec((tm, DK_R), lambda i, j: (i % pos_blocks, 0)),
        ],
        out_specs=pl.BlockSpec((tm, COL_TILE), lambda i, j: (i, j)),
        out_shape=jax.ShapeDtypeStruct((t, IN_COLS), F32),
        scratch_shapes=[pltpu.VMEM((tm, D_MODEL), BF16)],
        compiler_params=_cparams(("parallel", "arbitrary")),
        name="inproj",
    )(x, n1, w_bf, colw, cos_t, sin_t)


def _attn_scores(q, kp, kc, bias_p, bias_c, prev_ok):
    qi = lax.broadcasted_iota(jnp.int32, (ATTN_BLK, ATTN_BLK), 0)
    ki = lax.broadcasted_iota(jnp.int32, (ATTN_BLK, ATTN_BLK), 1)
    qb = q.astype(BF16)
    sp = _dot_nt(qb, kp.astype(BF16)) + bias_p
    sc = _dot_nt(qb, kc.astype(BF16)) + bias_c
    sp = jnp.where(jnp.logical_and(ki >= qi, prev_ok), sp, NEG)
    sc = jnp.where(ki <= qi, sc, NEG)
    return sp, sc


def _attn_softmax(sp, sc):
    m = jnp.maximum(jnp.max(sp, axis=-1, keepdims=True), jnp.max(sc, axis=-1, keepdims=True))
    pp = jnp.exp(sp - m)
    pc = jnp.exp(sc - m)
    den = jnp.sum(pp, axis=-1, keepdims=True) + jnp.sum(pc, axis=-1, keepdims=True)
    return pp.astype(BF16), pc.astype(BF16), den, m + jnp.log(den)


def _attn_prompt_kernel(*refs):
    q_refs, kc_refs, vc_refs, kp_refs, vp_refs = (refs[3 * n:3 * n + 3] for n in range(5))
    bias_ref, a_ref, o_scr, l_scr = refs[15:]
    i = pl.program_id(2)
    for g, (_, d) in enumerate(ATTN_GROUPS):
        span = ATTN_BLK * d
        nblk = ATTN_ROWS // span
        q_ref, kc_ref, vc_ref, kp_ref, vp_ref = q_refs[g], kc_refs[g], vc_refs[g], kp_refs[g], vp_refs[g]

        def tiles(it, carry, g=g, d=d, span=span, nblk=nblk,
                  q_ref=q_ref, kc_ref=kc_ref, vc_ref=vc_ref, kp_ref=kp_ref, vp_ref=vp_ref):
            where = []
            for u in range(ATTN_TILES_PER_TRIP):
                idx = it * ATTN_TILES_PER_TRIP + u
                r = idx % d
                c = idx // d
                rows = pl.ds(c * span + r, ATTN_BLK, stride=d)
                edge = pl.ds(r, ATTN_BLK, stride=d)
                inner = pl.ds(jnp.maximum(c - 1, 0) * span + r, ATTN_BLK, stride=d)
                where.append((c, rows, edge, inner))

            def prev_block(cur_ref, edge_ref, c, edge, inner):
                if nblk == 1:
                    return edge_ref[edge, :]
                return jnp.where(c == 0, edge_ref[edge, :], cur_ref[inner, :])

            scores = []
            for c, rows, edge, inner in where:
                prev_ok = jnp.logical_or(c > 0, i > 0)
                scores.append(_attn_scores(q_ref[rows, :], prev_block(kc_ref, kp_ref, c, edge, inner), kc_ref[rows, :],
                                           bias_ref[g, :, :ATTN_BLK], bias_ref[g, :, ATTN_BLK:], prev_ok))
            probs = [_attn_softmax(sp, sc) for sp, sc in scores]
            for (c, rows, edge, inner), (pp, pc, den, lse) in zip(where, probs):
                vp = prev_block(vc_ref, vp_ref, c, edge, inner)
                acc = _dot(pp, vp.astype(BF16)) + _dot(pc, vc_ref[rows, :].astype(BF16))
                o_scr[g, rows, :] = acc / den
                l_scr[g, rows, :] = jnp.broadcast_to(lse, (ATTN_BLK, HD_A))
            return carry

        lax.fori_loop(0, d * nblk // ATTN_TILES_PER_TRIP, tiles, 0)

    def combine(c, carry):
        rows = pl.ds(pl.multiple_of(c * ATTN_BLK, ATTN_BLK), ATTN_BLK)
        la, lb, lc = l_scr[0, rows, :], l_scr[1, rows, :], l_scr[2, rows, :]
        m = jnp.maximum(jnp.maximum(la, lb), lc)
        ea, eb, ec = jnp.exp(la - m), jnp.exp(lb - m), jnp.exp(lc - m)
        a_ref[rows, :] = (ea * o_scr[0, rows, :] + eb * o_scr[1, rows, :] + ec * o_scr[2, rows, :]) / (ea + eb + ec)
        return carry

    lax.fori_loop(0, ATTN_ROWS // ATTN_BLK, combine, 0)


def _attn_prompt(z3, bias, b, s):
    col = lambda base, g: (lambda bb, h, i, c0=(base + g * HEADS * HD_A) // HD_A: (bb, i, c0 + h))

    def prev_spec(base, g, d):
        span = ATTN_BLK * d
        nblk = ATTN_ROWS // span
        c0 = (base + g * HEADS * HD_A) // HD_A
        return pl.BlockSpec((None, span, HD_A), lambda bb, h, i: (bb, jnp.maximum(i * nblk - 1, 0), c0 + h))

    cur = lambda base, g: pl.BlockSpec((None, ATTN_ROWS, HD_A), col(base, g))
    groups = range(N_GROUPS)
    in_specs = ([cur(0, g) for g in groups] + [cur(QA_W, g) for g in groups] + [cur(2 * QA_W, g) for g in groups]
                + [prev_spec(QA_W, g, d) for g, (_, d) in enumerate(ATTN_GROUPS)]
                + [prev_spec(2 * QA_W, g, d) for g, (_, d) in enumerate(ATTN_GROUPS)]
                + [pl.BlockSpec((N_GROUPS, None, ATTN_BLK, 2 * ATTN_BLK), lambda bb, h, i: (0, h, 0, 0))])
    a = pl.pallas_call(
        _attn_prompt_kernel,
        grid=(b, HEADS, s // ATTN_ROWS),
        in_specs=in_specs,
        out_specs=pl.BlockSpec((None, ATTN_ROWS, HD_A), lambda bb, h, i: (bb, i, h)),
        out_shape=jax.ShapeDtypeStruct((b, s, HEADS * HD_A), F32),
        scratch_shapes=[pltpu.VMEM((N_GROUPS, ATTN_ROWS, HD_A), F32), pltpu.VMEM((N_GROUPS, ATTN_ROWS, HD_A), F32)],
        compiler_params=_cparams(("parallel", "parallel", "arbitrary")),
        name="attn_prompt",
    )(*([z3] * 15), bias)
    return a.reshape(b * s, HEADS * HD_A)


def _attn_sample_kernel(q_ref, k_ref, v_ref, c0_ref, c1_ref, c2_ref, brow_ref, bself_ref, a_ref):
    caches = (c0_ref, c1_ref, c2_ref)
    gw = HEADS * HD_A
    for h in range(HEADS):
        outs, lses = [], []
        for g in range(N_GROUPS):
            col = g * gw + h * HD_A
            q = q_ref[:, col:col + HD_A]
            kn = k_ref[:, col:col + HD_A]
            vn = v_ref[:, col:col + HD_A]
            kc = caches[g][:, 0, h, :]
            vc = caches[g][:, 1, h, :]
            sc = jnp.sum(kc * q, axis=-1, keepdims=True) + brow_ref[g, :, h:h + 1]
            s0 = jnp.sum(kn * q, axis=-1, keepdims=True) + bself_ref[g, :, h:h + 1][0:1]
            m = jnp.maximum(jnp.max(sc, axis=0, keepdims=True), s0)
            e = jnp.exp(sc - m)
            e0 = jnp.exp(s0 - m)
            den = jnp.sum(e, axis=0, keepdims=True) + e0
            outs.append((jnp.sum(e * vc, axis=0, keepdims=True) + e0 * vn) / den)
            lses.append(m + jnp.log(den))
        mm = jnp.maximum(jnp.maximum(lses[0], lses[1]), lses[2])
        ws = [jnp.exp(l - mm) for l in lses]
        wsum = ws[0] + ws[1] + ws[2]
        a_ref[:, h * HD_A:(h + 1) * HD_A] = (ws[0] * outs[0] + ws[1] * outs[1] + ws[2] * outs[2]) / wsum


def _attn_sample(qa, ka, va, caches, layer, brow, bself):
    db = qa.shape[0]
    gw = HEADS * HD_A
    cviews, cspecs = [], []
    for c, (window, d) in zip(caches, ATTN_GROUPS):
        assert c.shape[2] == window, "sample attention expects a full window of cached rows"
        cviews.append(c.reshape(c.shape[0], db, window // d, d, 2, HEADS, HD_A))
        cspecs.append(pl.BlockSpec((None, None, ATTN_BLK, None, 2, HEADS, HD_A), lambda b: (layer, b, 0, 0, 0, 0, 0)))
    tok = pl.BlockSpec((None, 1, QA_W), lambda b: (b, 0, 0))
    return pl.pallas_call(
        _attn_sample_kernel,
        grid=(db,),
        in_specs=[tok, tok, tok] + cspecs + [
            pl.BlockSpec((N_GROUPS, ATTN_BLK, HEADS), lambda b: (0, 0, 0)),
            pl.BlockSpec((N_GROUPS, 8, HEADS), lambda b: (0, 0, 0)),
        ],
        out_specs=pl.BlockSpec((None, 1, gw), lambda b: (b, 0, 0)),
        out_shape=jax.ShapeDtypeStruct((db, 1, gw), F32),
        compiler_params=_cparams(("parallel",)),
        name="attn_sample",
    )(qa, ka, va, *cviews, brow, bself)


def _ret_epilogue(o, gate):
    on = o * lax.rsqrt(jnp.mean(o * o, axis=-1, keepdims=True) + EPS)
    return on * (gate / (1.0 + jnp.exp(-gate)))


def _ret_prompt_kernel(q_ref, k_ref, va_ref, vb_ref, ga_ref, gb_ref, dec_ref, xi_ref, zeta_ref, o_ref, rout_ref, r_scr):
    c = pl.program_id(0)
    heads_per_tile = COL_TILE // DV_R
    nb = q_ref.shape[0]

    @pl.when(c == 0)
    def _():
        r_scr[...] = jnp.zeros_like(r_scr)

    pending = []
    for bb in range(nb):
        for h in range(HR):
            qs = slice(h * DK_R, (h + 1) * DK_R)
            ts = slice((h % heads_per_tile) * DV_R, (h % heads_per_tile + 1) * DV_R)
            v_ref, g_ref = ((va_ref, ga_ref), (vb_ref, gb_ref))[h // heads_per_tile]
            k = k_ref[bb, :, qs]
            qb = q_ref[bb, :, qs].astype(BF16)
            vb = v_ref[bb, :, ts].astype(BF16)
            r = r_scr[bb, h]
            xi = xi_ref[h]
            s = _dot_nt(qb, k.astype(BF16)) * dec_ref[h]
            cross = _dot(qb, r.astype(BF16)) * xi
            kz = (k * zeta_ref[h]).T
            r_scr[bb, h] = xi[RET_CHUNK - 1:RET_CHUNK, :] * r + _dot(kz.astype(BF16), vb)
            pending.append((bb, h, ts, g_ref, s.astype(BF16), vb, cross))
    for bb, h, ts, g_ref, sb, vb, cross in pending:
        o = _dot(sb, vb) + cross
        o_ref[bb, :, h * DV_R:(h + 1) * DV_R] = _ret_epilogue(o, g_ref[bb, :, ts])

    @pl.when(c == pl.num_programs(0) - 1)
    def _():
        rout_ref[...] = r_scr[...]


def _ret_prompt(z3, dec, xi, zeta, b, s):
    nc = s // RET_CHUNK
    qk_w = HR * DK_R
    tab = lambda w: pl.BlockSpec((HR, RET_CHUNK, w), lambda c: (0, 0, 0))
    col = lambda w, tile: pl.BlockSpec((b, RET_CHUNK, w), lambda c: (0, c, tile))
    o, r = pl.pallas_call(
        _ret_prompt_kernel,
        grid=(nc,),
        in_specs=[
            col(qk_w, CT_QR), col(qk_w, CT_KR),
            col(COL_TILE, CT_VR), col(COL_TILE, CT_VR + 1), col(COL_TILE, CT_GR), col(COL_TILE, CT_GR + 1),
            tab(RET_CHUNK), tab(DV_R), tab(DK_R),
        ],
        out_specs=[
            pl.BlockSpec((b, RET_CHUNK, RV_W), lambda c: (0, c, 0)),
            pl.BlockSpec((b, HR, DK_R, DV_R), lambda c: (0, 0, 0, 0)),
        ],
        out_shape=[jax.ShapeDtypeStruct((b, s, RV_W), F32), jax.ShapeDtypeStruct((b, HR, DK_R, DV_R), F32)],
        scratch_shapes=[pltpu.VMEM((b, HR, DK_R, DV_R), F32)],
        compiler_params=_cparams(("arbitrary",)),
        name="ret_prompt",
    )(z3, z3, z3, z3, z3, z3, dec, xi, zeta)
    return o.reshape(b * s, RV_W), r


def _ret_sample_kernel(q_ref, k_ref, v_ref, g_ref, r0_ref, gam_ref, o_ref, rout_ref):
    eye = (lax.broadcasted_iota(jnp.int32, (DK_R, DK_R), 0) == lax.broadcasted_iota(jnp.int32, (DK_R, DK_R), 1))
    for h in range(HR):
        qs = slice(h * DK_R, (h + 1) * DK_R)
        vs = slice(h * DV_R, (h + 1) * DV_R)
        q = q_ref[:, qs]
        k = k_ref[:, qs]
        v = v_ref[:, vs]
        r = r0_ref[h]
        gam = gam_ref[h]
        qcol = jnp.sum(jnp.where(eye, jnp.broadcast_to(q, (DK_R, DK_R)), 0.0), axis=-1, keepdims=True)
        kcol = jnp.sum(jnp.where(eye, jnp.broadcast_to(k, (DK_R, DK_R)), 0.0), axis=-1, keepdims=True)
        qk = jnp.sum(q * k, axis=-1, keepdims=True)
        o = qk * v + jnp.sum(qcol * r, axis=0, keepdims=True) * gam
        rout_ref[h] = gam * r + kcol * v
        o_ref[:, vs] = _ret_epilogue(o, g_ref[:, vs])


def _ret_sample(qr, kr, vr, gr, state, layer, gam):
    db = qr.shape[0]
    qk_w = HR * DK_R
    return pl.pallas_call(
        _ret_sample_kernel,
        grid=(db,),
        in_specs=[
            pl.BlockSpec((None, 1, qk_w), lambda b: (b, 0, 0)),
            pl.BlockSpec((None, 1, qk_w), lambda b: (b, 0, 0)),
            pl.BlockSpec((None, 1, RV_W), lambda b: (b, 0, 0)),
            pl.BlockSpec((None, 1, RV_W), lambda b: (b, 0, 0)),
            pl.BlockSpec((None, None, HR, DK_R, DV_R), lambda b: (layer, b, 0, 0, 0)),
            pl.BlockSpec((HR, 1, DV_R), lambda b: (0, 0, 0)),
        ],
        out_specs=[
            pl.BlockSpec((None, 1, RV_W), lambda b: (b, 0, 0)),
            pl.BlockSpec((None, HR, DK_R, DV_R), lambda b: (b, 0, 0, 0)),
        ],
        out_shape=[jax.ShapeDtypeStruct((db, 1, RV_W), F32), jax.ShapeDtypeStruct((db, HR, DK_R, DV_R), F32)],
        compiler_params=_cparams(("parallel",)),
        name="ret_sample",
    )(qr, kr, vr, gr, state, gam)


def _merge_kernel(a_ref, ro_ref, ga0, ga1, gb0, gb1, x_ref, woa_ref, wor_ref, wout_ref, wq_ref, bg_ref, n2_ref,
                  x1_ref, h2_ref, pq_ref):
    a = a_ref[...]
    ba = _dot(a.astype(BF16), woa_ref[...])
    br = _dot(ro_ref[...].astype(BF16), wor_ref[...])
    ga = jnp.concatenate([ga0[...], ga1[...]], axis=-1) + bg_ref[:, :D_MODEL]
    gb = jnp.concatenate([gb0[...], gb1[...]], axis=-1) + bg_ref[:, D_MODEL:]
    mixed = ba / (1.0 + jnp.exp(-ga)) + br / (1.0 + jnp.exp(-gb))
    x1 = x_ref[...] + _dot(mixed.astype(BF16), wout_ref[...])
    x1_ref[...] = x1
    h2 = (x1 * lax.rsqrt(jnp.mean(x1 * x1, axis=-1, keepdims=True) + EPS) * n2_ref[...]).astype(BF16)
    h2_ref[...] = h2
    pq_ref[...] = _dot(h2, wq_ref[...])


def _merge(a, ro, z, x, woa, wor, wout, wq, bg, n2, tm):
    t = x.shape[0]
    gw = HEADS * HD_A
    row = lambda w, c=0: pl.BlockSpec((tm, w), lambda i, c=c: (i, c))
    full = lambda a: pl.BlockSpec(a.shape, lambda i: (0, 0))
    in_specs = [
        row(gw), row(RV_W),
        row(COL_TILE, CT_GATE), row(COL_TILE, CT_GATE + 1), row(COL_TILE, CT_GATE + 2), row(COL_TILE, CT_GATE + 3),
        row(D_MODEL),
        full(woa), full(wor), full(wout), full(wq), full(bg), full(n2),
    ]
    return pl.pallas_call(
        _merge_kernel,
        grid=(t // tm,),
        in_specs=in_specs,
        out_specs=[row(D_MODEL), row(D_MODEL), row(wq.shape[1])],
        out_shape=[jax.ShapeDtypeStruct((t, D_MODEL), F32), jax.ShapeDtypeStruct((t, D_MODEL), BF16),
                   jax.ShapeDtypeStruct((t, wq.shape[1]), F32)],
        compiler_params=_cparams(("parallel",)),
        name="merge",
    )(a, ro, z, z, z, z, x, woa, wor, wout, wq, bg, n2)


def _topk_rank(s, break_ties):
    key = lax.broadcasted_iota(jnp.int32, s.shape, 0).astype(F32)
    rank = jnp.full(s.shape, float(PEER_TOPK), F32)
    vals = []
    for k in range(PEER_TOPK):
        m = jnp.max(s, axis=0, keepdims=True)
        hit = s == m
        if break_ties:
            hit = key == jnp.min(jnp.where(hit, key, float(N_KEYS)), axis=0, keepdims=True)
        rank = jnp.where(hit, float(k), rank)
        s = jnp.where(hit, -jnp.inf, s)
        vals.append(m)
    return rank, vals


def _exactly_16(picked):
    return jnp.where(jnp.sum(picked, axis=0, keepdims=True) != float(PEER_TOPK), 1.0, 0.0)


def _peer_select_head(pq_ref, sk_ref, row_ref, tile_ref, hd, break_ties):
    ntok = pq_ref.shape[0]
    hi = lax.Precision.HIGHEST
    sub = lax.broadcasted_iota(jnp.int32, (SUBLANES, ntok), 0).astype(F32)
    flat = jnp.concatenate([sub, 8.0 + sub, 16.0 + sub, 32.0 + sub, 48.0 + sub,
                            (8.0 + sub) * 16.0, sub * 16.0, sub * 16.0 + 1.0, sub * 16.0 + 2.0], axis=0)
    ninf = jnp.full((SUBLANES, ntok), -jnp.inf, F32)
    sc = []
    for p in range(2):
        col = (hd * 2 + p) * N_KEYS
        sc.append(_dot_nt(sk_ref[p], pq_ref[:, col:col + N_KEYS], hi))
    rank1, sv1 = _topk_rank(sc[0], break_ties)
    rank2, sv2 = _topk_rank(sc[1], break_ties)
    a_lo, a_hi = jnp.concatenate(sv1[:8], axis=0), jnp.concatenate(sv1[8:], axis=0)
    b_lo, b_hi = jnp.concatenate(sv2[:8], axis=0), jnp.concatenate(sv2[8:], axis=0)
    cand = jnp.concatenate([
        sv1[0] + b_lo, sv1[0] + b_hi, sv1[1] + b_lo,
        jnp.where(sub < 5.0, sv1[2] + b_lo, ninf),
        jnp.where(sub < 4.0, sv1[3] + b_lo, ninf),
        a_hi + sv2[0],
        jnp.where(sub >= 4.0, a_lo + sv2[0], ninf),
        jnp.where(sub >= 4.0, a_lo + sv2[1], ninf),
        jnp.where(sub == 4.0, a_lo + sv2[2], ninf),
    ], axis=0)
    work = cand
    chosen = jnp.zeros(cand.shape, F32)
    for _ in range(PEER_TOPK):
        m = jnp.max(work, axis=0, keepdims=True)
        hit = work == m
        if break_ties:
            hit = flat == jnp.min(jnp.where(hit, flat, float(PEER_TOPK * PEER_TOPK)), axis=0, keepdims=True)
        chosen = jnp.where(hit, 1.0, chosen)
        work = jnp.where(hit, -jnp.inf, work)
    top = sv1[0] + sv2[0]
    zsum = jnp.sum(chosen * jnp.exp(cand - top), axis=0, keepdims=True)
    cnt_rows = [jnp.sum(chosen[0:16], axis=0, keepdims=True)]
    cnt_rows += [jnp.sum(chosen[8 * g:8 * g + 8], axis=0, keepdims=True) for g in (2, 3, 4)]
    cnt_mid = chosen[48:56] + chosen[56:64] + chosen[64:72]
    cnt_rows += [cnt_mid[a:a + 1] for a in range(4, 8)]
    cnt_rows += [chosen[40 + a:41 + a] for a in range(8)]
    cnt1 = jnp.zeros((N_KEYS, ntok), F32)
    for a in range(PEER_TOPK):
        cnt1 = jnp.where(rank1 == float(a), cnt_rows[a], cnt1)
    row_ref[hd, 0] = cnt1
    row_ref[hd, 1] = 0.5 * jnp.exp(sc[0] - sv1[0]) / zsum
    tile_ref[hd, 0] = rank2
    tile_ref[hd, 1] = jnp.exp(sc[1] - sv2[0])
    if break_ties:
        return None
    in_top = lambda rank: jnp.where(rank < float(PEER_TOPK), 1.0, 0.0)
    return _exactly_16(in_top(rank1)) + _exactly_16(in_top(rank2)) + _exactly_16(chosen)


def _peer_select_kernel(pq_ref, sk_ref, row_ref, tile_ref):
    ties = None
    for hd in range(PEER_HEADS):
        t = _peer_select_head(pq_ref, sk_ref, row_ref, tile_ref, hd, break_ties=False)
        ties = t if ties is None else ties + t

    @pl.when(jnp.max(ties) > 0.0)
    def _():
        for hd in range(PEER_HEADS):
            _peer_select_head(pq_ref, sk_ref, row_ref, tile_ref, hd, break_ties=True)


def _peer_select(pq, sk):
    t = pq.shape[0]
    tl = PEER_TOK_LANES
    shape = (t // tl, PEER_HEADS, 2, N_KEYS, tl)
    spec = pl.BlockSpec((None, PEER_HEADS, 2, N_KEYS, tl), lambda i: (i, 0, 0, 0, 0))
    return pl.pallas_call(
        _peer_select_kernel,
        grid=(t // tl,),
        in_specs=[
            pl.BlockSpec((tl, pq.shape[1]), lambda i: (i, 0)),
            pl.BlockSpec(sk.shape, lambda i: (0, 0, 0)),
        ],
        out_specs=[spec, spec],
        out_shape=[jax.ShapeDtypeStruct(shape, F32), jax.ShapeDtypeStruct(shape, F32)],
        compiler_params=_cparams(("parallel",)),
        name="peer_select",
    )(pq, sk)


def _gelu_x2(x):
    return x * (1.0 + lax.erf(x * np.float32(2.0 ** -0.5)))


def _peer_dense_kernel(h_ref, row_ref, tile_ref, u_ref, vt_ref, x_ref, out_ref, act_ref, g_ref, acc_ref, *, nsub):
    e = pl.program_id(1)
    keys_per_blk = PEER_EXPERT_BLK // N_KEYS
    groups = N_KEYS // SUBLANES
    vreg = (SUBLANES, PEER_TOK_LANES)

    @pl.when(e == 0)
    def _():
        acc_ref[...] = jnp.zeros_like(acc_ref)

    act_ref[...] = _gelu_x2(_dot_nt(u_ref[...], h_ref[...]))

    def gate(kk, carry):
        i = e * keys_per_blk + kk
        row0 = pl.multiple_of(kk * N_KEYS, N_KEYS)
        for t in range(nsub):
            lanes = slice(t * PEER_TOK_LANES, (t + 1) * PEER_TOK_LANES)
            w = [None] * groups
            for hd in range(PEER_HEADS):
                cnt = jnp.broadcast_to(row_ref[t, hd, 0, pl.ds(i, 1), :], vreg)
                e1 = jnp.broadcast_to(row_ref[t, hd, 1, pl.ds(i, 1), :], vreg)
                for g in range(groups):
                    keys = slice(g * SUBLANES, (g + 1) * SUBLANES)
                    term = jnp.where(tile_ref[t, hd, 0, keys, :] < cnt, e1 * tile_ref[t, hd, 1, keys, :], 0.0)
                    w[g] = term if w[g] is None else w[g] + term
            rows = pl.ds(row0, N_KEYS)
            g_ref[rows, lanes] = (jnp.concatenate(w, axis=0) * act_ref[rows, lanes]).astype(BF16)
        return carry

    lax.fori_loop(0, keys_per_blk, gate, 0)
    acc_ref[...] += _dot(vt_ref[...], g_ref[...])

    @pl.when(e == pl.num_programs(1) - 1)
    def _():
        out_ref[...] = x_ref[...] + acc_ref[...].T


def _peer_dense(h2, sel_rows, sel_tiles, u_bf, vt_bf, x1, tb):
    t = h2.shape[0]
    nsub = tb // PEER_TOK_LANES
    eb = PEER_EXPERT_BLK
    sel_spec = pl.BlockSpec((nsub, PEER_HEADS, 2, N_KEYS, PEER_TOK_LANES), lambda i, e: (i, 0, 0, 0, 0))
    return pl.pallas_call(
        functools.partial(_peer_dense_kernel, nsub=nsub),
        grid=(t // tb, N_EXPERTS // eb),
        in_specs=[
            pl.BlockSpec((tb, D_MODEL), lambda i, e: (i, 0)),
            sel_spec,
            sel_spec,
            pl.BlockSpec((eb, D_MODEL), lambda i, e: (e, 0)),
            pl.BlockSpec((D_MODEL, eb), lambda i, e: (0, e)),
            pl.BlockSpec((tb, D_MODEL), lambda i, e: (i, 0)),
        ],
        out_specs=pl.BlockSpec((tb, D_MODEL), lambda i, e: (i, 0)),
        out_shape=jax.ShapeDtypeStruct((t, D_MODEL), F32),
        scratch_shapes=[
            pltpu.VMEM((eb, tb), F32),
            pltpu.VMEM((eb, tb), BF16),
            pltpu.VMEM((D_MODEL, tb), F32),
        ],
        compiler_params=_cparams(("parallel", "arbitrary")),
        name="peer_dense",
    )(h2, sel_rows, sel_tiles, u_bf, vt_bf, x1)


def _peer(h2, pq, x1, sk, u_bf, vt_bf):
    t = h2.shape[0]
    tb = min(512, -(-t // PEER_TOK_LANES) * PEER_TOK_LANES)
    t_pad = -(-t // tb) * tb
    if t_pad != t:
        h2 = jnp.pad(h2, ((0, t_pad - t), (0, 0)))
        x1 = jnp.pad(x1, ((0, t_pad - t), (0, 0)))
        pq = jnp.pad(pq, ((0, t_pad - t), (0, 0)))
    sel_rows, sel_tiles = _peer_select(pq, sk)
    return _peer_dense(h2, sel_rows, sel_tiles, u_bf, vt_bf, x1, tb)[:t]


def _t5_bucket(dist):
    dist = np.asarray(dist, np.int64)
    max_exact = N_BUCKETS // 2
    ratio = np.log(np.maximum(dist, 1) / max_exact) / np.log(REL_MAX_DIST / max_exact)
    large = np.minimum(max_exact + (ratio * (N_BUCKETS - max_exact)).astype(np.int64), N_BUCKETS - 1)
    return np.where(dist < max_exact, dist, large).astype(np.int32)


def _rotary_tables(pos):
    half = DK_R // 2
    ang = jnp.asarray(1.0 / (10000.0 ** np.linspace(0.0, 1.0, half)), F32)
    theta = pos.astype(F32)[:, None] * ang[None, :]
    c, s = jnp.cos(theta), jnp.sin(theta)
    return jnp.concatenate([c, c], axis=-1), jnp.concatenate([-s, s], axis=-1)


def _retention_tables(length, log_g):
    n = np.arange(length)
    diff = n[:, None] - n[None, :]
    decay = jnp.where(jnp.asarray(diff >= 0)[None],
                      jnp.exp(jnp.asarray(np.maximum(diff, 0), F32)[None] * log_g[:, None, None]), 0.0)
    xi = jnp.exp(jnp.asarray(n + 1, F32)[None, :] * log_g[:, None])
    zeta = jnp.exp(jnp.asarray(n[::-1].copy(), F32)[None, :] * log_g[:, None])
    xi = jnp.broadcast_to(xi[:, :, None], (HR, length, DV_R))
    zeta = jnp.broadcast_to(zeta[:, :, None], (HR, length, DK_R))
    return decay, xi, zeta


def _prompt_bias(bias_g, g, d):
    steps = ATTN_GROUPS[g][0] // d
    assert steps == ATTN_BLK
    per_dist = bias_g[:, g][_t5_bucket(np.arange(steps + 1) * d)]
    n = 3 * ATTN_BLK
    diag = per_dist[np.clip(2 * ATTN_BLK - 1 - np.arange(n), 0, steps)].T
    rows = jnp.tile(diag, (1, ATTN_BLK + 1))[:, :ATTN_BLK * (n + 1)].reshape(HEADS, ATTN_BLK, n + 1)
    return rows[:, ::-1, :2 * ATTN_BLK]


def _sample_bias(bias_g):
    rows, selfs = [], []
    for g, (window, d) in enumerate(ATTN_GROUPS):
        steps = window // d
        assert steps == ATTN_BLK
        j = steps - np.arange(steps)
        rows.append(bias_g[:, g][_t5_bucket(j * d)])
        selfs.append(jnp.broadcast_to(bias_g[:, g][_t5_bucket(np.zeros(1))], (8, HEADS)))
    return jnp.stack(rows), jnp.stack(selfs)


def kernel(x_prompt, x_sample, cache_kv_d1, cache_kv_d4, cache_kv_d16, state_ret, rel_bias, norm1_w, w_in, b_gate, q_norm_w, k_norm_w, w_o_attn, w_o_ret, w_out, norm2_w, peer_w_q, peer_sub_keys, peer_u, peer_v):
    b, s, _ = x_prompt.shape
    db, ds, _ = x_sample.shape
    assert ds == 1, "the sample group is one new token per sequence"
    depth = w_in.shape[0]
    gw = HEADS * HD_A
    caches = (cache_kv_d1, cache_kv_d4, cache_kv_d16)

    bias_g = rel_bias.reshape(N_BUCKETS, N_GROUPS, HEADS).astype(F32)
    log_g = jnp.asarray(np.log(1.0 - 2.0 ** (-5.0 - np.arange(HR))), F32)
    cos_p, sin_p = _rotary_tables(jnp.arange(s, dtype=jnp.int32))
    cos_s, sin_s = _rotary_tables(jnp.full((db,), PAST_LEN, jnp.int32))
    dec_p, xi_p, zeta_p = _retention_tables(RET_CHUNK, log_g)
    gam_s = _retention_tables(1, log_g)[1]
    bias_p = jnp.stack([_prompt_bias(bias_g, g, d) for g, (_, d) in enumerate(ATTN_GROUPS)])
    brow_s, bself_s = _sample_bias(bias_g)

    assert s % ATTN_ROWS == 0 and s % RET_CHUNK == 0, "prompt length must be a multiple of the attention step"
    tm_p = next(t for t in (2048, 1024, 512, ATTN_BLK) if s % t == 0)
    xp = x_prompt.reshape(b * s, D_MODEL)
    xs = x_sample.reshape(db, D_MODEL)
    kv_p = [[] for _ in range(N_GROUPS)]
    kv_s = [[] for _ in range(N_GROUPS)]
    ret_p, ret_s = [], []

    for l in range(depth):
        w_in_bf = w_in[l].astype(BF16)
        ones = jnp.ones((COL_TILE,), F32)
        colw = jnp.stack(
            [jnp.tile(q_norm_w[l, g], HEADS) * (HD_A ** -0.5) for g in range(N_GROUPS)]
            + [jnp.tile(k_norm_w[l, g], HEADS) for g in range(N_GROUPS)]
            + [ones] * (CT_KR - CT_V) + [ones * (DK_R ** -0.5)] + [ones] * (N_COL_TILES - CT_KR - 1)
        ).reshape(N_COL_TILES, 1, COL_TILE)
        n1 = norm1_w[l].reshape(1, D_MODEL)
        n2 = norm2_w[l].reshape(1, D_MODEL)
        bg = b_gate[l].reshape(1, 2 * D_MODEL)
        woa, wor, wout = w_o_attn[l].astype(BF16), w_o_ret[l].astype(BF16), w_out[l].astype(BF16)
        wq = peer_w_q[l].astype(BF16)
        sk = peer_sub_keys[l]
        u_bf = peer_u[l].astype(BF16)
        vt_bf = peer_v[l].T.astype(BF16)

        zp = _inproj(xp, n1, w_in_bf, colw, cos_p, sin_p, tm_p, s // tm_p)
        zp3 = zp.reshape(b, s, IN_COLS)
        a_p = _attn_prompt(zp3, bias_p, b, s)
        for g, (window, d) in enumerate(ATTN_GROUPS):
            keep = min(window, s)
            k_rows = zp3[:, s - keep:, QA_W + g * gw:QA_W + (g + 1) * gw].reshape(b, keep, HEADS, HD_A)
            v_rows = zp3[:, s - keep:, 2 * QA_W + g * gw:2 * QA_W + (g + 1) * gw].reshape(b, keep, HEADS, HD_A)
            kv_p[g].append(jnp.stack([k_rows, v_rows], axis=2))
        ro, r_new = _ret_prompt(zp3, dec_p, xi_p, zeta_p, b, s)
        ret_p.append(r_new)
        x1, h2, pq = _merge(a_p, ro, zp, xp, woa, wor, wout, wq, bg, n2, 256 if (b * s) % 256 == 0 else ATTN_BLK)
        xp = _peer(h2, pq, x1, sk, u_bf, vt_bf)

        zs = _inproj(xs, n1, w_in_bf, colw, cos_s, sin_s, db, 1)
        zs3 = zs.reshape(db, 1, IN_COLS)
        qa_s, ka_s, va_s = zs3[:, :, :QA_W], zs3[:, :, QA_W:2 * QA_W], zs3[:, :, 2 * QA_W:3 * QA_W]
        c0 = 3 * QA_W
        qr_s, kr_s = zs3[:, :, c0:c0 + RQ_W], zs3[:, :, c0 + RQ_W:c0 + 2 * RQ_W]
        c1 = c0 + 2 * RQ_W
        vr_s, gr_s = zs3[:, :, c1:c1 + RV_W], zs3[:, :, c1 + RV_W:c1 + 2 * RV_W]
        a_s = _attn_sample(qa_s, ka_s, va_s, caches, l, brow_s, bself_s)
        for g in range(N_GROUPS):
            k_rows = ka_s[:, :, g * gw:(g + 1) * gw].reshape(db, 1, HEADS, HD_A)
            v_rows = va_s[:, :, g * gw:(g + 1) * gw].reshape(db, 1, HEADS, HD_A)
            kv_s[g].append(jnp.stack([k_rows, v_rows], axis=2))
        ro_s, r_s = _ret_sample(qr_s, kr_s, vr_s, gr_s, state_ret, l, gam_s)
        ret_s.append(r_s)
        x1s, h2s, pqs = _merge(a_s.reshape(db, gw), ro_s.reshape(db, RV_W), zs, xs, woa, wor, wout, wq, bg, n2, db)
        xs = _peer(h2s, pqs, x1s, sk, u_bf, vt_bf)

    return (xp.reshape(b, s, D_MODEL), xs.reshape(db, 1, D_MODEL),
            jnp.stack(kv_p[0]), jnp.stack(kv_p[1]), jnp.stack(kv_p[2]), jnp.stack(ret_p),
            jnp.stack(kv_s[0]), jnp.stack(kv_s[1]), jnp.stack(kv_s[2]), jnp.stack(ret_s))
```

```python
import functools

import numpy as np
import jax
import jax.numpy as jnp
from jax import lax
from jax.experimental import pallas as pl
from jax.experimental.pallas import tpu as pltpu

F32 = jnp.float32
BF16 = jnp.bfloat16

D_MODEL = 1024
PAST_LEN = 16384
ATTN_GROUPS = ((128, 1), (512, 4), (2048, 16))
N_GROUPS = 3
HEADS = 4
HD_A = 128
ATTN_BLK = 128
ATTN_ROWS = 2048
ATTN_TILES_PER_TRIP = 8
N_BUCKETS = 32
REL_MAX_DIST = 2048
HR = 4
DK_R = 128
DV_R = 256
RET_CHUNK = 128
N_KEYS = 128
N_EXPERTS = N_KEYS * N_KEYS
PEER_HEADS = 8
PEER_TOPK = 16
EPS = 1e-6
NEG = -1e30

QA_W = N_GROUPS * HEADS * HD_A
RQ_W = HR * DK_R
RV_W = HR * DV_R
IN_COLS = 3 * QA_W + 2 * RQ_W + 2 * RV_W + 2 * D_MODEL
COL_TILE = 512
N_COL_TILES = IN_COLS // COL_TILE
CT_K = QA_W // COL_TILE
CT_V = 2 * QA_W // COL_TILE
CT_QR = 3 * QA_W // COL_TILE
CT_KR = CT_QR + 1
CT_VR = CT_KR + 1
CT_GR = CT_VR + 2
CT_GATE = CT_GR + 2

LANES = 128
SUBLANES = 8
VMEM_LIMIT = 56 * 1024 * 1024

PEER_TOK_LANES = 128
PEER_EXPERT_BLK = 2048


def _dot(a, b, precision=None):
    return jnp.dot(a, b, preferred_element_type=F32, precision=precision)


def _dot_nt(a, b, precision=None):
    return lax.dot_general(a, b, (((1,), (1,)), ((), ())), preferred_element_type=F32, precision=precision)


def _cparams(sem):
    return pltpu.CompilerParams(dimension_semantics=sem, vmem_limit_bytes=VMEM_LIMIT)


def _inproj_kernel(x_ref, n1_ref, w_ref, colw_ref, cos_ref, sin_ref, z_ref, hn_ref):
    j = pl.program_id(1)

    @pl.when(j == 0)
    def _():
        x = x_ref[...]
        ms = jnp.mean(x * x, axis=-1, keepdims=True)
        hn_ref[...] = (x * lax.rsqrt(ms + EPS) * n1_ref[...]).astype(BF16)

    y = _dot(hn_ref[...], w_ref[...])
    cw = colw_ref[...]

    @pl.when(j < CT_V)
    def _():
        for h in range(HEADS):
            sl = slice(h * HD_A, (h + 1) * HD_A)
            yh = y[:, sl]
            ms = jnp.mean(yh * yh, axis=-1, keepdims=True)
            z_ref[:, sl] = yh * lax.rsqrt(ms + EPS) * cw[:, sl]

    @pl.when((j == CT_QR) | (j == CT_KR))
    def _():
        c = cos_ref[...]
        s = sin_ref[...]
        for h in range(HR):
            sl = slice(h * DK_R, (h + 1) * DK_R)
            yh = y[:, sl]
            z_ref[:, sl] = (yh * c + pltpu.roll(yh, DK_R // 2, 1) * s) * cw[:, sl]

    @pl.when(((j >= CT_V) & (j < CT_QR)) | (j >= CT_VR))
    def _():
        z_ref[...] = y


def _inproj(x, n1, w_bf, colw, cos_t, sin_t, tm, pos_blocks):
    t = x.shape[0]
    return pl.pallas_call(
        _inproj_kernel,
        grid=(t // tm, N_COL_TILES),
        in_specs=[
            pl.BlockSpec((tm, D_MODEL), lambda i, j: (i, 0)),
            pl.BlockSpec((1, D_MODEL), lambda i, j: (0, 0)),
            pl.BlockSpec((D_MODEL, COL_TILE), lambda i, j: (0, j)),
            pl.BlockSpec((None, 1, COL_TILE), lambda i, j: (j, 0, 0)),
            pl.BlockSpec((tm, DK_R), lambda i, j: (i % pos_blocks, 0)),
            pl.BlockSpec((tm, DK_R), lambda i, j: (i % pos_blocks, 0)),
        ],
        out_specs=pl.BlockSpec((tm, COL_TILE), lambda i, j: (i, j)),
        out_shape=jax.ShapeDtypeStruct((t, IN_COLS), F32),
        scratch_shapes=[pltpu.VMEM((tm, D_MODEL), BF16)],
        compiler_params=_cparams(("parallel", "arbitrary")),
        name="inproj",
    )(x, n1, w_bf, colw, cos_t, sin_t)


def _attn_scores(q, kp, kc, bias_p, bias_c, prev_ok):
    qi = lax.broadcasted_iota(jnp.int32, (ATTN_BLK, ATTN_BLK), 0)
    ki = lax.broadcasted_iota(jnp.int32, (ATTN_BLK, ATTN_BLK), 1)
    qb = q.astype(BF16)
    sp = _dot_nt(qb, kp.astype(BF16)) + bias_p
    sc = _dot_nt(qb, kc.astype(BF16)) + bias_c
    sp = jnp.where(jnp.logical_and(ki >= qi, prev_ok), sp, NEG)
    sc = jnp.where(ki <= qi, sc, NEG)
    return sp, sc


def _attn_softmax(sp, sc):
    m = jnp.maximum(jnp.max(sp, axis=-1, keepdims=True), jnp.max(sc, axis=-1, keepdims=True))
    pp = jnp.exp(sp - m)
    pc = jnp.exp(sc - m)
    den = jnp.sum(pp, axis=-1, keepdims=True) + jnp.sum(pc, axis=-1, keepdims=True)
    return pp.astype(BF16), pc.astype(BF16), den, m + jnp.log(den)


def _attn_prompt_kernel(*refs):
    q_refs, kc_refs, vc_refs, kp_refs, vp_refs = (refs[3 * n:3 * n + 3] for n in range(5))
    bias_ref, a_ref, o_scr, l_scr = refs[15:]
    i = pl.program_id(2)
    for g, (_, d) in enumerate(ATTN_GROUPS):
        span = ATTN_BLK * d
        nblk = ATTN_ROWS // span
        q_ref, kc_ref, vc_ref, kp_ref, vp_ref = q_refs[g], kc_refs[g], vc_refs[g], kp_refs[g], vp_refs[g]

        def tiles(it, carry, g=g, d=d, span=span, nblk=nblk,
                  q_ref=q_ref, kc_ref=kc_ref, vc_ref=vc_ref, kp_ref=kp_ref, vp_ref=vp_ref):
            where = []
            for u in range(ATTN_TILES_PER_TRIP):
                idx = it * ATTN_TILES_PER_TRIP + u
                r = idx % d
                c = idx // d
                rows = pl.ds(c * span + r, ATTN_BLK, stride=d)
                edge = pl.ds(r, ATTN_BLK, stride=d)
                inner = pl.ds(jnp.maximum(c - 1, 0) * span + r, ATTN_BLK, stride=d)
                where.append((c, rows, edge, inner))

            def prev_block(cur_ref, edge_ref, c, edge, inner):
                if nblk == 1:
                    return edge_ref[edge, :]
                return jnp.where(c == 0, edge_ref[edge, :], cur_ref[inner, :])

            scores = []
            for c, rows, edge, inner in where:
                prev_ok = jnp.logical_or(c > 0, i > 0)
                scores.append(_attn_scores(q_ref[rows, :], prev_block(kc_ref, kp_ref, c, edge, inner), kc_ref[rows, :],
                                           bias_ref[g, :, :ATTN_BLK], bias_ref[g, :, ATTN_BLK:], prev_ok))
            probs = [_attn_softmax(sp, sc) for sp, sc in scores]
            for (c, rows, edge, inner), (pp, pc, den, lse) in zip(where, probs):
                vp = prev_block(vc_ref, vp_ref, c, edge, inner)
                acc = _dot(pp, vp.astype(BF16)) + _dot(pc, vc_ref[rows, :].astype(BF16))
                o_scr[g, rows, :] = acc / den
                l_scr[g, rows, :] = jnp.broadcast_to(lse, (ATTN_BLK, HD_A))
            return carry

        lax.fori_loop(0, d * nblk // ATTN_TILES_PER_TRIP, tiles, 0)

    def combine(c, carry):
        rows = pl.ds(pl.multiple_of(c * ATTN_BLK, ATTN_BLK), ATTN_BLK)
        la, lb, lc = l_scr[0, rows, :], l_scr[1, rows, :], l_scr[2, rows, :]
        m = jnp.maximum(jnp.maximum(la, lb), lc)
        ea, eb, ec = jnp.exp(la - m), jnp.exp(lb - m), jnp.exp(lc - m)
        a_ref[rows, :] = (ea * o_scr[0, rows, :] + eb * o_scr[1, rows, :] + ec * o_scr[2, rows, :]) / (ea + eb + ec)
        return carry

    lax.fori_loop(0, ATTN_ROWS // ATTN_BLK, combine, 0)


def _attn_prompt(z3, bias, b, s):
    col = lambda base, g: (lambda bb, h, i, c0=(base + g * HEADS * HD_A) // HD_A: (bb, i, c0 + h))

    def prev_spec(base, g, d):
        span = ATTN_BLK * d
        nblk = ATTN_ROWS // span
        c0 = (base + g * HEADS * HD_A) // HD_A
        return pl.BlockSpec((None, span, HD_A), lambda bb, h, i: (bb, jnp.maximum(i * nblk - 1, 0), c0 + h))

    cur = lambda base, g: pl.BlockSpec((None, ATTN_ROWS, HD_A), col(base, g))
    groups = range(N_GROUPS)
    in_specs = ([cur(0, g) for g in groups] + [cur(QA_W, g) for g in groups] + [cur(2 * QA_W, g) for g in groups]
                + [prev_spec(QA_W, g, d) for g, (_, d) in enumerate(ATTN_GROUPS)]
                + [prev_spec(2 * QA_W, g, d) for g, (_, d) in enumerate(ATTN_GROUPS)]
                + [pl.BlockSpec((N_GROUPS, None, ATTN_BLK, 2 * ATTN_BLK), lambda bb, h, i: (0, h, 0, 0))])
    a = pl.pallas_call(
        _attn_prompt_kernel,
        grid=(b, HEADS, s // ATTN_ROWS),
        in_specs=in_specs,
        out_specs=pl.BlockSpec((None, ATTN_ROWS, HD_A), lambda bb, h, i: (bb, i, h)),
        out_shape=jax.ShapeDtypeStruct((b, s, HEADS * HD_A), F32),
        scratch_shapes=[pltpu.VMEM((N_GROUPS, ATTN_ROWS, HD_A), F32), pltpu.VMEM((N_GROUPS, ATTN_ROWS, HD_A), F32)],
        compiler_params=_cparams(("parallel", "parallel", "arbitrary")),
        name="attn_prompt",
    )(*([z3] * 15), bias)
    return a.reshape(b * s, HEADS * HD_A)


def _attn_sample_kernel(q_ref, k_ref, v_ref, c0_ref, c1_ref, c2_ref, brow_ref, bself_ref, a_ref):
    caches = (c0_ref, c1_ref, c2_ref)
    gw = HEADS * HD_A
    for h in range(HEADS):
        outs, lses = [], []
        for g in range(N_GROUPS):
            col = g * gw + h * HD_A
            q = q_ref[:, col:col + HD_A]
            kn = k_ref[:, col:col + HD_A]
            vn = v_ref[:, col:col + HD_A]
            kc = caches[g][:, 0, h, :]
            vc = caches[g][:, 1, h, :]
            sc = jnp.sum(kc * q, axis=-1, keepdims=True) + brow_ref[g, :, h:h + 1]
            s0 = jnp.sum(kn * q, axis=-1, keepdims=True) + bself_ref[g, :, h:h + 1][0:1]
            m = jnp.maximum(jnp.max(sc, axis=0, keepdims=True), s0)
            e = jnp.exp(sc - m)
            e0 = jnp.exp(s0 - m)
            den = jnp.sum(e, axis=0, keepdims=True) + e0
            outs.append((jnp.sum(e * vc, axis=0, keepdims=True) + e0 * vn) / den)
            lses.append(m + jnp.log(den))
        mm = jnp.maximum(jnp.maximum(lses[0], lses[1]), lses[2])
        ws = [jnp.exp(l - mm) for l in lses]
        wsum = ws[0] + ws[1] + ws[2]
        a_ref[:, h * HD_A:(h + 1) * HD_A] = (ws[0] * outs[0] + ws[1] * outs[1] + ws[2] * outs[2]) / wsum


def _attn_sample(qa, ka, va, caches, layer, brow, bself):
    db = qa.shape[0]
    gw = HEADS * HD_A
    cviews, cspecs = [], []
    for c, (window, d) in zip(caches, ATTN_GROUPS):
        assert c.shape[2] == window, "sample attention expects a full window of cached rows"
        cviews.append(c.reshape(c.shape[0], db, window // d, d, 2, HEADS, HD_A))
        cspecs.append(pl.BlockSpec((None, None, ATTN_BLK, None, 2, HEADS, HD_A), lambda b: (layer, b, 0, 0, 0, 0, 0)))
    tok = pl.BlockSpec((None, 1, QA_W), lambda b: (b, 0, 0))
    return pl.pallas_call(
        _attn_sample_kernel,
        grid=(db,),
        in_specs=[tok, tok, tok] + cspecs + [
            pl.BlockSpec((N_GROUPS, ATTN_BLK, HEADS), lambda b: (0, 0, 0)),
            pl.BlockSpec((N_GROUPS, 8, HEADS), lambda b: (0, 0, 0)),
        ],
        out_specs=pl.BlockSpec((None, 1, gw), lambda b: (b, 0, 0)),
        out_shape=jax.ShapeDtypeStruct((db, 1, gw), F32),
        compiler_params=_cparams(("parallel",)),
        name="attn_sample",
    )(qa, ka, va, *cviews, brow, bself)


def _ret_epilogue(o, gate):
    on = o * lax.rsqrt(jnp.mean(o * o, axis=-1, keepdims=True) + EPS)
    return on * (gate / (1.0 + jnp.exp(-gate)))


def _ret_prompt_kernel(q_ref, k_ref, va_ref, vb_ref, ga_ref, gb_ref, dec_ref, xi_ref, zeta_ref, o_ref, rout_ref, r_scr):
    c = pl.program_id(0)
    heads_per_tile = COL_TILE // DV_R
    nb = q_ref.shape[0]

    @pl.when(c == 0)
    def _():
        r_scr[...] = jnp.zeros_like(r_scr)

    pending = []
    for bb in range(nb):
        for h in range(HR):
            qs = slice(h * DK_R, (h + 1) * DK_R)
            ts = slice((h % heads_per_tile) * DV_R, (h % heads_per_tile + 1) * DV_R)
            v_ref, g_ref = ((va_ref, ga_ref), (vb_ref, gb_ref))[h // heads_per_tile]
            k = k_ref[bb, :, qs]
            qb = q_ref[bb, :, qs].astype(BF16)
            vb = v_ref[bb, :, ts].astype(BF16)
            r = r_scr[bb, h]
            xi = xi_ref[h]
            s = _dot_nt(qb, k.astype(BF16)) * dec_ref[h]
            cross = _dot(qb, r.astype(BF16)) * xi
            kz = (k * zeta_ref[h]).T
            r_scr[bb, h] = xi[RET_CHUNK - 1:RET_CHUNK, :] * r + _dot(kz.astype(BF16), vb)
            pending.append((bb, h, ts, g_ref, s.astype(BF16), vb, cross))
    for bb, h, ts, g_ref, sb, vb, cross in pending:
        o = _dot(sb, vb) + cross
        o_ref[bb, :, h * DV_R:(h + 1) * DV_R] = _ret_epilogue(o, g_ref[bb, :, ts])

    @pl.when(c == pl.num_programs(0) - 1)
    def _():
        rout_ref[...] = r_scr[...]


def _ret_prompt(z3, dec, xi, zeta, b, s):
    nc = s // RET_CHUNK
    qk_w = HR * DK_R
    tab = lambda w: pl.BlockSpec((HR, RET_CHUNK, w), lambda c: (0, 0, 0))
    col = lambda w, tile: pl.BlockSpec((b, RET_CHUNK, w), lambda c: (0, c, tile))
    o, r = pl.pallas_call(
        _ret_prompt_kernel,
        grid=(nc,),
        in_specs=[
            col(qk_w, CT_QR), col(qk_w, CT_KR),
            col(COL_TILE, CT_VR), col(COL_TILE, CT_VR + 1), col(COL_TILE, CT_GR), col(COL_TILE, CT_GR + 1),
            tab(RET_CHUNK), tab(DV_R), tab(DK_R),
        ],
        out_specs=[
            pl.BlockSpec((b, RET_CHUNK, RV_W), lambda c: (0, c, 0)),
            pl.BlockSpec((b, HR, DK_R, DV_R), lambda c: (0, 0, 0, 0)),
        ],
        out_shape=[jax.ShapeDtypeStruct((b, s, RV_W), F32), jax.ShapeDtypeStruct((b, HR, DK_R, DV_R), F32)],
        scratch_shapes=[pltpu.VMEM((b, HR, DK_R, DV_R), F32)],
        compiler_params=_cparams(("arbitrary",)),
        name="ret_prompt",
    )(z3, z3, z3, z3, z3, z3, dec, xi, zeta)
    return o.reshape(b * s, RV_W), r


def _ret_sample_kernel(q_ref, k_ref, v_ref, g_ref, r0_ref, gam_ref, o_ref, rout_ref):
    eye = (lax.broadcasted_iota(jnp.int32, (DK_R, DK_R), 0) == lax.broadcasted_iota(jnp.int32, (DK_R, DK_R), 1))
    for h in range(HR):
        qs = slice(h * DK_R, (h + 1) * DK_R)
        vs = slice(h * DV_R, (h + 1) * DV_R)
        q = q_ref[:, qs]
        k = k_ref[:, qs]
        v = v_ref[:, vs]
        r = r0_ref[h]
        gam = gam_ref[h]
        qcol = jnp.sum(jnp.where(eye, jnp.broadcast_to(q, (DK_R, DK_R)), 0.0), axis=-1, keepdims=True)
        kcol = jnp.sum(jnp.where(eye, jnp.broadcast_to(k, (DK_R, DK_R)), 0.0), axis=-1, keepdims=True)
        qk = jnp.sum(q * k, axis=-1, keepdims=True)
        o = qk * v + jnp.sum(qcol * r, axis=0, keepdims=True) * gam
        rout_ref[h] = gam * r + kcol * v
        o_ref[:, vs] = _ret_epilogue(o, g_ref[:, vs])


def _ret_sample(qr, kr, vr, gr, state, layer, gam):
    db = qr.shape[0]
    qk_w = HR * DK_R
    return pl.pallas_call(
        _ret_sample_kernel,
        grid=(db,),
        in_specs=[
            pl.BlockSpec((None, 1, qk_w), lambda b: (b, 0, 0)),
            pl.BlockSpec((None, 1, qk_w), lambda b: (b, 0, 0)),
            pl.BlockSpec((None, 1, RV_W), lambda b: (b, 0, 0)),
            pl.BlockSpec((None, 1, RV_W), lambda b: (b, 0, 0)),
            pl.BlockSpec((None, None, HR, DK_R, DV_R), lambda b: (layer, b, 0, 0, 0)),
            pl.BlockSpec((HR, 1, DV_R), lambda b: (0, 0, 0)),
        ],
        out_specs=[
            pl.BlockSpec((None, 1, RV_W), lambda b: (b, 0, 0)),
            pl.BlockSpec((None, HR, DK_R, DV_R), lambda b: (b, 0, 0, 0)),
        ],
        out_shape=[jax.ShapeDtypeStruct((db, 1, RV_W), F32), jax.ShapeDtypeStruct((db, HR, DK_R, DV_R), F32)],
        compiler_params=_cparams(("parallel",)),
        name="ret_sample",
    )(qr, kr, vr, gr, state, gam)


def _merge_kernel(a_ref, ro_ref, ga0, ga1, gb0, gb1, x_ref, woa_ref, wor_ref, wout_ref, wq_ref, bg_ref, n2_ref,
                  x1_ref, h2_ref, pq_ref):
    a = a_ref[...]
    ba = _dot(a.astype(BF16), woa_ref[...])
    br = _dot(ro_ref[...].astype(BF16), wor_ref[...])
    ga = jnp.concatenate([ga0[...], ga1[...]], axis=-1) + bg_ref[:, :D_MODEL]
    gb = jnp.concatenate([gb0[...], gb1[...]], axis=-1) + bg_ref[:, D_MODEL:]
    mixed = ba / (1.0 + jnp.exp(-ga)) + br / (1.0 + jnp.exp(-gb))
    x1 = x_ref[...] + _dot(mixed.astype(BF16), wout_ref[...])
    x1_ref[...] = x1
    h2 = (x1 * lax.rsqrt(jnp.mean(x1 * x1, axis=-1, keepdims=True) + EPS) * n2_ref[...]).astype(BF16)
    h2_ref[...] = h2
    pq_ref[...] = _dot(h2, wq_ref[...])


def _merge(a, ro, z, x, woa, wor, wout, wq, bg, n2, tm):
    t = x.shape[0]
    gw = HEADS * HD_A
    row = lambda w, c=0: pl.BlockSpec((tm, w), lambda i, c=c: (i, c))
    full = lambda a: pl.BlockSpec(a.shape, lambda i: (0, 0))
    in_specs = [
        row(gw), row(RV_W),
        row(COL_TILE, CT_GATE), row(COL_TILE, CT_GATE + 1), row(COL_TILE, CT_GATE + 2), row(COL_TILE, CT_GATE + 3),
        row(D_MODEL),
        full(woa), full(wor), full(wout), full(wq), full(bg), full(n2),
    ]
    return pl.pallas_call(
        _merge_kernel,
        grid=(t // tm,),
        in_specs=in_specs,
        out_specs=[row(D_MODEL), row(D_MODEL), row(wq.shape[1])],
        out_shape=[jax.ShapeDtypeStruct((t, D_MODEL), F32), jax.ShapeDtypeStruct((t, D_MODEL), BF16),
                   jax.ShapeDtypeStruct((t, wq.shape[1]), F32)],
        compiler_params=_cparams(("parallel",)),
        name="merge",
    )(a, ro, z, z, z, z, x, woa, wor, wout, wq, bg, n2)


def _topk_rank(s, break_ties):
    key = lax.broadcasted_iota(jnp.int32, s.shape, 0).astype(F32)
    rank = jnp.full(s.shape, float(PEER_TOPK), F32)
    vals = []
    for k in range(PEER_TOPK):
        m = jnp.max(s, axis=0, keepdims=True)
        hit = s == m
        if break_ties:
            hit = key == jnp.min(jnp.where(hit, key, float(N_KEYS)), axis=0, keepdims=True)
        rank = jnp.where(hit, float(k), rank)
        s = jnp.where(hit, -jnp.inf, s)
        vals.append(m)
    return rank, vals


def _exactly_16(picked):
    return jnp.where(jnp.sum(picked, axis=0, keepdims=True) != float(PEER_TOPK), 1.0, 0.0)


def _peer_select_head(pq_ref, sk_ref, row_ref, tile_ref, hd, break_ties):
    ntok = pq_ref.shape[0]
    hi = lax.Precision.HIGHEST
    sub = lax.broadcasted_iota(jnp.int32, (SUBLANES, ntok), 0).astype(F32)
    flat = jnp.concatenate([sub, 8.0 + sub, 16.0 + sub, 32.0 + sub, 48.0 + sub,
                            (8.0 + sub) * 16.0, sub * 16.0, sub * 16.0 + 1.0, sub * 16.0 + 2.0], axis=0)
    ninf = jnp.full((SUBLANES, ntok), -jnp.inf, F32)
    sc = []
    for p in range(2):
        col = (hd * 2 + p) * N_KEYS
        sc.append(_dot_nt(sk_ref[p], pq_ref[:, col:col + N_KEYS], hi))
    rank1, sv1 = _topk_rank(sc[0], break_ties)
    rank2, sv2 = _topk_rank(sc[1], break_ties)
    a_lo, a_hi = jnp.concatenate(sv1[:8], axis=0), jnp.concatenate(sv1[8:], axis=0)
    b_lo, b_hi = jnp.concatenate(sv2[:8], axis=0), jnp.concatenate(sv2[8:], axis=0)
    cand = jnp.concatenate([
        sv1[0] + b_lo, sv1[0] + b_hi, sv1[1] + b_lo,
        jnp.where(sub < 5.0, sv1[2] + b_lo, ninf),
        jnp.where(sub < 4.0, sv1[3] + b_lo, ninf),
        a_hi + sv2[0],
        jnp.where(sub >= 4.0, a_lo + sv2[0], ninf),
        jnp.where(sub >= 4.0, a_lo + sv2[1], ninf),
        jnp.where(sub == 4.0, a_lo + sv2[2], ninf),
    ], axis=0)
    work = cand
    chosen = jnp.zeros(cand.shape, F32)
    for _ in range(PEER_TOPK):
        m = jnp.max(work, axis=0, keepdims=True)
        hit = work == m
        if break_ties:
            hit = flat == jnp.min(jnp.where(hit, flat, float(PEER_TOPK * PEER_TOPK)), axis=0, keepdims=True)
        chosen = jnp.where(hit, 1.0, chosen)
        work = jnp.where(hit, -jnp.inf, work)
    top = sv1[0] + sv2[0]
    zsum = jnp.sum(chosen * jnp.exp(cand - top), axis=0, keepdims=True)
    cnt_rows = [jnp.sum(chosen[0:16], axis=0, keepdims=True)]
    cnt_rows += [jnp.sum(chosen[8 * g:8 * g + 8], axis=0, keepdims=True) for g in (2, 3, 4)]
    cnt_mid = chosen[48:56] + chosen[56:64] + chosen[64:72]
    cnt_rows += [cnt_mid[a:a + 1] for a in range(4, 8)]
    cnt_rows += [chosen[40 + a:41 + a] for a in range(8)]
    cnt1 = jnp.zeros((N_KEYS, ntok), F32)
    for a in range(PEER_TOPK):
        cnt1 = jnp.where(rank1 == float(a), cnt_rows[a], cnt1)
    row_ref[hd, 0] = cnt1
    row_ref[hd, 1] = 0.5 * jnp.exp(sc[0] - sv1[0]) / zsum
    tile_ref[hd, 0] = rank2
    tile_ref[hd, 1] = jnp.exp(sc[1] - sv2[0])
    if break_ties:
        return None
    in_top = lambda rank: jnp.where(rank < float(PEER_TOPK), 1.0, 0.0)
    return _exactly_16(in_top(rank1)) + _exactly_16(in_top(rank2)) + _exactly_16(chosen)


def _peer_select_kernel(pq_ref, sk_ref, row_ref, tile_ref):
    ties = None
    for hd in range(PEER_HEADS):
        t = _peer_select_head(pq_ref, sk_ref, row_ref, tile_ref, hd, break_ties=False)
        ties = t if ties is None else ties + t

    @pl.when(jnp.max(ties) > 0.0)
    def _():
        for hd in range(PEER_HEADS):
            _peer_select_head(pq_ref, sk_ref, row_ref, tile_ref, hd, break_ties=True)


def _peer_select(pq, sk):
    t = pq.shape[0]
    tl = PEER_TOK_LANES
    shape = (t // tl, PEER_HEADS, 2, N_KEYS, tl)
    spec = pl.BlockSpec((None, PEER_HEADS, 2, N_KEYS, tl), lambda i: (i, 0, 0, 0, 0))
    return pl.pallas_call(
        _peer_select_kernel,
        grid=(t // tl,),
        in_specs=[
            pl.BlockSpec((tl, pq.shape[1]), lambda i: (i, 0)),
            pl.BlockSpec(sk.shape, lambda i: (0, 0, 0)),
        ],
        out_specs=[spec, spec],
        out_shape=[jax.ShapeDtypeStruct(shape, F32), jax.ShapeDtypeStruct(shape, F32)],
        compiler_params=_cparams(("parallel",)),
        name="peer_select",
    )(pq, sk)


def _gelu_x2(x):
    return x * (1.0 + lax.erf(x * np.float32(2.0 ** -0.5)))


def _peer_dense_kernel(h_ref, row_ref, tile_ref, u_ref, vt_ref, x_ref, out_ref, act_ref, g_ref, acc_ref, *, nsub):
    e = pl.program_id(1)
    keys_per_blk = PEER_EXPERT_BLK // N_KEYS
    groups = N_KEYS // SUBLANES
    vreg = (SUBLANES, PEER_TOK_LANES)

    @pl.when(e == 0)
    def _():
        acc_ref[...] = jnp.zeros_like(acc_ref)

    act_ref[...] = _gelu_x2(_dot_nt(u_ref[...], h_ref[...]))

    def gate(kk, carry):
        i = e * keys_per_blk + kk
        row0 = pl.multiple_of(kk * N_KEYS, N_KEYS)
        for t in range(nsub):
            lanes = slice(t * PEER_TOK_LANES, (t + 1) * PEER_TOK_LANES)
            w = [None] * groups
            for hd in range(PEER_HEADS):
                cnt = jnp.broadcast_to(row_ref[t, hd, 0, pl.ds(i, 1), :], vreg)
                e1 = jnp.broadcast_to(row_ref[t, hd, 1, pl.ds(i, 1), :], vreg)
                for g in range(groups):
                    keys = slice(g * SUBLANES, (g + 1) * SUBLANES)
                    term = jnp.where(tile_ref[t, hd, 0, keys, :] < cnt, e1 * tile_ref[t, hd, 1, keys, :], 0.0)
                    w[g] = term if w[g] is None else w[g] + term
            rows = pl.ds(row0, N_KEYS)
            g_ref[rows, lanes] = (jnp.concatenate(w, axis=0) * act_ref[rows, lanes]).astype(BF16)
        return carry

    lax.fori_loop(0, keys_per_blk, gate, 0)
    acc_ref[...] += _dot(vt_ref[...], g_ref[...])

    @pl.when(e == pl.num_programs(1) - 1)
    def _():
        out_ref[...] = x_ref[...] + acc_ref[...].T


def _peer_dense(h2, sel_rows, sel_tiles, u_bf, vt_bf, x1, tb):
    t = h2.shape[0]
    nsub = tb // PEER_TOK_LANES
    eb = PEER_EXPERT_BLK
    sel_spec = pl.BlockSpec((nsub, PEER_HEADS, 2, N_KEYS, PEER_TOK_LANES), lambda i, e: (i, 0, 0, 0, 0))
    return pl.pallas_call(
        functools.partial(_peer_dense_kernel, nsub=nsub),
        grid=(t // tb, N_EXPERTS // eb),
        in_specs=[
            pl.BlockSpec((tb, D_MODEL), lambda i, e: (i, 0)),
            sel_spec,
            sel_spec,
            pl.BlockSpec((eb, D_MODEL), lambda i, e: (e, 0)),
            pl.BlockSpec((D_MODEL, eb), lambda i, e: (0, e)),
            pl.BlockSpec((tb, D_MODEL), lambda i, e: (i, 0)),
        ],
        out_specs=pl.BlockSpec((tb, D_MODEL), lambda i, e: (i, 0)),
        out_shape=jax.ShapeDtypeStruct((t, D_MODEL), F32),
        scratch_shapes=[
            pltpu.VMEM((eb, tb), F32),
            pltpu.VMEM((eb, tb), BF16),
            pltpu.VMEM((D_MODEL, tb), F32),
        ],
        compiler_params=_cparams(("parallel", "arbitrary")),
        name="peer_dense",
    )(h2, sel_rows, sel_tiles, u_bf, vt_bf, x1)


def _peer(h2, pq, x1, sk, u_bf, vt_bf):
    t = h2.shape[0]
    tb = min(512, -(-t // PEER_TOK_LANES) * PEER_TOK_LANES)
    t_pad = -(-t // tb) * tb
    if t_pad != t:
        h2 = jnp.pad(h2, ((0, t_pad - t), (0, 0)))
        x1 = jnp.pad(x1, ((0, t_pad - t), (0, 0)))
        pq = jnp.pad(pq, ((0, t_pad - t), (0, 0)))
    sel_rows, sel_tiles = _peer_select(pq, sk)
    return _peer_dense(h2, sel_rows, sel_tiles, u_bf, vt_bf, x1, tb)[:t]


def _t5_bucket(dist):
    dist = np.asarray(dist, np.int64)
    max_exact = N_BUCKETS // 2
    ratio = np.log(np.maximum(dist, 1) / max_exact) / np.log(REL_MAX_DIST / max_exact)
    large = np.minimum(max_exact + (ratio * (N_BUCKETS - max_exact)).astype(np.int64), N_BUCKETS - 1)
    return np.where(dist < max_exact, dist, large).astype(np.int32)


def _rotary_tables(pos):
    half = DK_R // 2
    ang = jnp.asarray(1.0 / (10000.0 ** np.linspace(0.0, 1.0, half)), F32)
    theta = pos.astype(F32)[:, None] * ang[None, :]
    c, s = jnp.cos(theta), jnp.sin(theta)
    return jnp.concatenate([c, c], axis=-1), jnp.concatenate([-s, s], axis=-1)


def _retention_tables(length, log_g):
    n = np.arange(length)
    diff = n[:, None] - n[None, :]
    decay = jnp.where(jnp.asarray(diff >= 0)[None],
                      jnp.exp(jnp.asarray(np.maximum(diff, 0), F32)[None] * log_g[:, None, None]), 0.0)
    xi = jnp.exp(jnp.asarray(n + 1, F32)[None, :] * log_g[:, None])
    zeta = jnp.exp(jnp.asarray(n[::-1].copy(), F32)[None, :] * log_g[:, None])
    xi = jnp.broadcast_to(xi[:, :, None], (HR, length, DV_R))
    zeta = jnp.broadcast_to(zeta[:, :, None], (HR, length, DK_R))
    return decay, xi, zeta


def _prompt_bias(bias_g, g, d):
    steps = ATTN_GROUPS[g][0] // d
    assert steps == ATTN_BLK
    per_dist = bias_g[:, g][_t5_bucket(np.arange(steps + 1) * d)]
    n = 3 * ATTN_BLK
    diag = per_dist[np.clip(2 * ATTN_BLK - 1 - np.arange(n), 0, steps)].T
    rows = jnp.tile(diag, (1, ATTN_BLK + 1))[:, :ATTN_BLK * (n + 1)].reshape(HEADS, ATTN_BLK, n + 1)
    return rows[:, ::-1, :2 * ATTN_BLK]


def _sample_bias(bias_g):
    rows, selfs = [], []
    for g, (window, d) in enumerate(ATTN_GROUPS):
        steps = window // d
        assert steps == ATTN_BLK
        j = steps - np.arange(steps)
        rows.append(bias_g[:, g][_t5_bucket(j * d)])
        selfs.append(jnp.broadcast_to(bias_g[:, g][_t5_bucket(np.zeros(1))], (8, HEADS)))
    return jnp.stack(rows), jnp.stack(selfs)


def kernel(x_prompt, x_sample, cache_kv_d1, cache_kv_d4, cache_kv_d16, state_ret, rel_bias, norm1_w, w_in, b_gate, q_norm_w, k_norm_w, w_o_attn, w_o_ret, w_out, norm2_w, peer_w_q, peer_sub_keys, peer_u, peer_v):
    b, s, _ = x_prompt.shape
    db, ds, _ = x_sample.shape
    assert ds == 1, "the sample group is one new token per sequence"
    depth = w_in.shape[0]
    gw = HEADS * HD_A
    caches = (cache_kv_d1, cache_kv_d4, cache_kv_d16)

    bias_g = rel_bias.reshape(N_BUCKETS, N_GROUPS, HEADS).astype(F32)
    log_g = jnp.asarray(np.log(1.0 - 2.0 ** (-5.0 - np.arange(HR))), F32)
    cos_p, sin_p = _rotary_tables(jnp.arange(s, dtype=jnp.int32))
    cos_s, sin_s = _rotary_tables(jnp.full((db,), PAST_LEN, jnp.int32))
    dec_p, xi_p, zeta_p = _retention_tables(RET_CHUNK, log_g)
    gam_s = _retention_tables(1, log_g)[1]
    bias_p = jnp.stack([_prompt_bias(bias_g, g, d) for g, (_, d) in enumerate(ATTN_GROUPS)])
    brow_s, bself_s = _sample_bias(bias_g)

    assert s % ATTN_ROWS == 0 and s % RET_CHUNK == 0, "prompt length must be a multiple of the attention step"
    tm_p = next(t for t in (2048, 1024, 512, ATTN_BLK) if s % t == 0)
    xp = x_prompt.reshape(b * s, D_MODEL)
    xs = x_sample.reshape(db, D_MODEL)
    kv_p = [[] for _ in range(N_GROUPS)]
    kv_s = [[] for _ in range(N_GROUPS)]
    ret_p, ret_s = [], []

    for l in range(depth):
        w_in_bf = w_in[l].astype(BF16)
        ones = jnp.ones((COL_TILE,), F32)
        colw = jnp.stack(
            [jnp.tile(q_norm_w[l, g], HEADS) * (HD_A ** -0.5) for g in range(N_GROUPS)]
            + [jnp.tile(k_norm_w[l, g], HEADS) for g in range(N_GROUPS)]
            + [ones] * (CT_KR - CT_V) + [ones * (DK_R ** -0.5)] + [ones] * (N_COL_TILES - CT_KR - 1)
        ).reshape(N_COL_TILES, 1, COL_TILE)
        n1 = norm1_w[l].reshape(1, D_MODEL)
        n2 = norm2_w[l].reshape(1, D_MODEL)
        bg = b_gate[l].reshape(1, 2 * D_MODEL)
        woa, wor, wout = w_o_attn[l].astype(BF16), w_o_ret[l].astype(BF16), w_out[l].astype(BF16)
        wq = peer_w_q[l].astype(BF16)
        sk = peer_sub_keys[l]
        u_bf = peer_u[l].astype(BF16)
        vt_bf = peer_v[l].T.astype(BF16)

        zp = _inproj(xp, n1, w_in_bf, colw, cos_p, sin_p, tm_p, s // tm_p)
        zp3 = zp.reshape(b, s, IN_COLS)
        a_p = _attn_prompt(zp3, bias_p, b, s)
        for g, (window, d) in enumerate(ATTN_GROUPS):
            keep = min(window, s)
            k_rows = zp3[:, s - keep:, QA_W + g * gw:QA_W + (g + 1) * gw].reshape(b, keep, HEADS, HD_A)
            v_rows = zp3[:, s - keep:, 2 * QA_W + g * gw:2 * QA_W + (g + 1) * gw].reshape(b, keep, HEADS, HD_A)
            kv_p[g].append(jnp.stack([k_rows, v_rows], axis=2))
        ro, r_new = _ret_prompt(zp3, dec_p, xi_p, zeta_p, b, s)
        ret_p.append(r_new)
        x1, h2, pq = _merge(a_p, ro, zp, xp, woa, wor, wout, wq, bg, n2, 256 if (b * s) % 256 == 0 else ATTN_BLK)
        xp = _peer(h2, pq, x1, sk, u_bf, vt_bf)

        zs = _inproj(xs, n1, w_in_bf, colw, cos_s, sin_s, db, 1)
        zs3 = zs.reshape(db, 1, IN_COLS)
        qa_s, ka_s, va_s = zs3[:, :, :QA_W], zs3[:, :, QA_W:2 * QA_W], zs3[:, :, 2 * QA_W:3 * QA_W]
        c0 = 3 * QA_W
        qr_s, kr_s = zs3[:, :, c0:c0 + RQ_W], zs3[:, :, c0 + RQ_W:c0 + 2 * RQ_W]
        c1 = c0 + 2 * RQ_W
        vr_s, gr_s = zs3[:, :, c1:c1 + RV_W], zs3[:, :, c1 + RV_W:c1 + 2 * RV_W]
        a_s = _attn_sample(qa_s, ka_s, va_s, caches, l, brow_s, bself_s)
        for g in range(N_GROUPS):
            k_rows = ka_s[:, :, g * gw:(g + 1) * gw].reshape(db, 1, HEADS, HD_A)
            v_rows = va_s[:, :, g * gw:(g + 1) * gw].reshape(db, 1, HEADS, HD_A)
            kv_s[g].append(jnp.stack([k_rows, v_rows], axis=2))
        ro_s, r_s = _ret_sample(qr_s, kr_s, vr_s, gr_s, state_ret, l, gam_s)
        ret_s.append(r_s)
        x1s, h2s, pqs = _merge(a_s.reshape(db, gw), ro_s.reshape(db, RV_W), zs, xs, woa, wor, wout, wq, bg, n2, db)
        xs = _peer(h2s, pqs, x1s, sk, u_bf, vt_bf)

    return (xp.reshape(b, s, D_MODEL), xs.reshape(db, 1, D_MODEL),
            jnp.stack(kv_p[0]), jnp.stack(kv_p[1]), jnp.stack(kv_p[2]), jnp.stack(ret_p),
            jnp.stack(kv_s[0]), jnp.stack(kv_s[1]), jnp.stack(kv_s[2]), jnp.stack(ret_s))
```

```python
import functools

import numpy as np
import jax
import jax.numpy as jnp
from jax import lax
from jax.experimental import pallas as pl
from jax.experimental.pallas import tpu as pltpu

F32 = jnp.float32
BF16 = jnp.bfloat16

D_MODEL = 1024
PAST_LEN = 16384
ATTN_GROUPS = ((128, 1), (512, 4), (2048, 16))
N_GROUPS = 3
HEADS = 4
HD_A = 128
ATTN_BLK = 128
ATTN_ROWS = 2048
ATTN_TILES_PER_TRIP = 8
N_BUCKETS = 32
REL_MAX_DIST = 2048
HR = 4
DK_R = 128
DV_R = 256
RET_CHUNK = 128
N_KEYS = 128
N_EXPERTS = N_KEYS * N_KEYS
PEER_HEADS = 8
PEER_TOPK = 16
EPS = 1e-6
NEG = -1e30

QA_W = N_GROUPS * HEADS * HD_A
RQ_W = HR * DK_R
RV_W = HR * DV_R
IN_COLS = 3 * QA_W + 2 * RQ_W + 2 * RV_W + 2 * D_MODEL
COL_TILE = 512
N_COL_TILES = IN_COLS // COL_TILE
CT_V = 2 * QA_W // COL_TILE
CT_QR = 3 * QA_W // COL_TILE
CT_KR = CT_QR + 1
CT_VR = CT_KR + 1
CT_GR = CT_VR + 2
CT_GATE = CT_GR + 2

SUBLANES = 8
VMEM_LIMIT = 56 * 1024 * 1024

PEER_TOK_LANES = 128
PEER_EXPERT_BLK = 2048


def _dot(a, b, precision=None):
    return jnp.dot(a, b, preferred_element_type=F32, precision=precision)


def _dot_nt(a, b, precision=None):
    return lax.dot_general(a, b, (((1,), (1,)), ((), ())), preferred_element_type=F32, precision=precision)


def _cparams(sem):
    return pltpu.CompilerParams(dimension_semantics=sem, vmem_limit_bytes=VMEM_LIMIT)


def _inproj_kernel(x_ref, n1_ref, w_ref, colw_ref, cos_ref, sin_ref, z_ref, hn_ref):
    j = pl.program_id(1)

    @pl.when(j == 0)
    def _():
        x = x_ref[...]
        ms = jnp.mean(x * x, axis=-1, keepdims=True)
        hn_ref[...] = (x * lax.rsqrt(ms + EPS) * n1_ref[...]).astype(BF16)

    y = _dot(hn_ref[...], w_ref[...])
    cw = colw_ref[...]

    @pl.when(j < CT_V)
    def _():
        for h in range(HEADS):
            sl = slice(h * HD_A, (h + 1) * HD_A)
            yh = y[:, sl]
            ms = jnp.mean(yh * yh, axis=-1, keepdims=True)
            z_ref[:, sl] = yh * lax.rsqrt(ms + EPS) * cw[:, sl]

    @pl.when((j == CT_QR) | (j == CT_KR))
    def _():
        c = cos_ref[...]
        s = sin_ref[...]
        for h in range(HR):
            sl = slice(h * DK_R, (h + 1) * DK_R)
            yh = y[:, sl]
            z_ref[:, sl] = (yh * c + pltpu.roll(yh, DK_R // 2, 1) * s) * cw[:, sl]

    @pl.when(((j >= CT_V) & (j < CT_QR)) | (j >= CT_VR))
    def _():
        z_ref[...] = y


def _inproj(x, n1, w_bf, colw, cos_t, sin_t, tm, pos_blocks):
    t = x.shape[0]
    return pl.pallas_call(
        _inproj_kernel,
        grid=(t // tm, N_COL_TILES),
        in_specs=[
            pl.BlockSpec((tm, D_MODEL), lambda i, j: (i, 0)),
            pl.BlockSpec((1, D_MODEL), lambda i, j: (0, 0)),
            pl.BlockSpec((D_MODEL, COL_TILE), lambda i, j: (0, j)),
            pl.BlockSpec((None, 1, COL_TILE), lambda i, j: (j, 0, 0)),
            pl.BlockSpec((tm, DK_R), lambda i, j: (i % pos_blocks, 0)),
            pl.BlockSpec((tm, DK_R), lambda i, j: (i % pos_blocks, 0)),
        ],
        out_specs=pl.BlockSpec((tm, COL_TILE), lambda i, j: (i, j)),
        out_shape=jax.ShapeDtypeStruct((t, IN_COLS), F32),
        scratch_shapes=[pltpu.VMEM((tm, D_MODEL), BF16)],
        compiler_params=_cparams(("parallel", "arbitrary")),
        name="inproj",
    )(x, n1, w_bf, colw, cos_t, sin_t)


def _attn_scores(q, kp, kc, bias_p, bias_c, prev_ok):
    qi = lax.broadcasted_iota(jnp.int32, (ATTN_BLK, ATTN_BLK), 0)
    ki = lax.broadcasted_iota(jnp.int32, (ATTN_BLK, ATTN_BLK), 1)
    qb = q.astype(BF16)
    sp = _dot_nt(qb, kp.astype(BF16)) + bias_p
    sc = _dot_nt(qb, kc.astype(BF16)) + bias_c
    sp = jnp.where(jnp.logical_and(ki >= qi, prev_ok), sp, NEG)
    sc = jnp.where(ki <= qi, sc, NEG)
    return sp, sc


def _attn_softmax(sp, sc):
    m = jnp.maximum(jnp.max(sp, axis=-1, keepdims=True), jnp.max(sc, axis=-1, keepdims=True))
    pp = jnp.exp(sp - m)
    pc = jnp.exp(sc - m)
    den = jnp.sum(pp, axis=-1, keepdims=True) + jnp.sum(pc, axis=-1, keepdims=True)
    return pp.astype(BF16), pc.astype(BF16), den, m + jnp.log(den)


def _attn_prompt_kernel(*refs):
    q_refs, kc_refs, vc_refs, kp_refs, vp_refs = (refs[3 * n:3 * n + 3] for n in range(5))
    bias_ref, a_ref, o_scr, l_scr = refs[15:]
    i = pl.program_id(2)
    for g, (_, d) in enumerate(ATTN_GROUPS):
        span = ATTN_BLK * d
        nblk = ATTN_ROWS // span
        q_ref, kc_ref, vc_ref, kp_ref, vp_ref = q_refs[g], kc_refs[g], vc_refs[g], kp_refs[g], vp_refs[g]

        def tiles(it, carry, g=g, d=d, span=span, nblk=nblk,
                  q_ref=q_ref, kc_ref=kc_ref, vc_ref=vc_ref, kp_ref=kp_ref, vp_ref=vp_ref):
            where = []
            for u in range(ATTN_TILES_PER_TRIP):
                idx = it * ATTN_TILES_PER_TRIP + u
                r = idx % d
                c = idx // d
                rows = pl.ds(c * span + r, ATTN_BLK, stride=d)
                edge = pl.ds(r, ATTN_BLK, stride=d)
                inner = pl.ds(jnp.maximum(c - 1, 0) * span + r, ATTN_BLK, stride=d)
                where.append((c, rows, edge, inner))

            def prev_block(cur_ref, edge_ref, c, edge, inner):
                if nblk == 1:
                    return edge_ref[edge, :]
                return jnp.where(c == 0, edge_ref[edge, :], cur_ref[inner, :])

            scores = []
            for c, rows, edge, inner in where:
                prev_ok = jnp.logical_or(c > 0, i > 0)
                scores.append(_attn_scores(q_ref[rows, :], prev_block(kc_ref, kp_ref, c, edge, inner), kc_ref[rows, :],
                                           bias_ref[g, :, :ATTN_BLK], bias_ref[g, :, ATTN_BLK:], prev_ok))
            probs = [_attn_softmax(sp, sc) for sp, sc in scores]
            for (c, rows, edge, inner), (pp, pc, den, lse) in zip(where, probs):
                vp = prev_block(vc_ref, vp_ref, c, edge, inner)
                acc = _dot(pp, vp.astype(BF16)) + _dot(pc, vc_ref[rows, :].astype(BF16))
                o_scr[g, rows, :] = acc / den
                l_scr[g, rows, :] = jnp.broadcast_to(lse, (ATTN_BLK, HD_A))
            return carry

        lax.fori_loop(0, d * nblk // ATTN_TILES_PER_TRIP, tiles, 0)

    def combine(c, carry):
        rows = pl.ds(pl.multiple_of(c * ATTN_BLK, ATTN_BLK), ATTN_BLK)
        la, lb, lc = l_scr[0, rows, :], l_scr[1, rows, :], l_scr[2, rows, :]
        m = jnp.maximum(jnp.maximum(la, lb), lc)
        ea, eb, ec = jnp.exp(la - m), jnp.exp(lb - m), jnp.exp(lc - m)
        a_ref[rows, :] = (ea * o_scr[0, rows, :] + eb * o_scr[1, rows, :] + ec * o_scr[2, rows, :]) / (ea + eb + ec)
        return carry

    lax.fori_loop(0, ATTN_ROWS // ATTN_BLK, combine, 0)


def _attn_prompt(z3, bias, b, s):
    col = lambda base, g: (lambda bb, h, i, c0=(base + g * HEADS * HD_A) // HD_A: (bb, i, c0 + h))

    def prev_spec(base, g, d):
        span = ATTN_BLK * d
        nblk = ATTN_ROWS // span
        c0 = (base + g * HEADS * HD_A) // HD_A
        return pl.BlockSpec((None, span, HD_A), lambda bb, h, i: (bb, jnp.maximum(i * nblk - 1, 0), c0 + h))

    cur = lambda base, g: pl.BlockSpec((None, ATTN_ROWS, HD_A), col(base, g))
    groups = range(N_GROUPS)
    in_specs = ([cur(0, g) for g in groups] + [cur(QA_W, g) for g in groups] + [cur(2 * QA_W, g) for g in groups]
                + [prev_spec(QA_W, g, d) for g, (_, d) in enumerate(ATTN_GROUPS)]
                + [prev_spec(2 * QA_W, g, d) for g, (_, d) in enumerate(ATTN_GROUPS)]
                + [pl.BlockSpec((N_GROUPS, None, ATTN_BLK, 2 * ATTN_BLK), lambda bb, h, i: (0, h, 0, 0))])
    a = pl.pallas_call(
        _attn_prompt_kernel,
        grid=(b, HEADS, s // ATTN_ROWS),
        in_specs=in_specs,
        out_specs=pl.BlockSpec((None, ATTN_ROWS, HD_A), lambda bb, h, i: (bb, i, h)),
        out_shape=jax.ShapeDtypeStruct((b, s, HEADS * HD_A), F32),
        scratch_shapes=[pltpu.VMEM((N_GROUPS, ATTN_ROWS, HD_A), F32), pltpu.VMEM((N_GROUPS, ATTN_ROWS, HD_A), F32)],
        compiler_params=_cparams(("parallel", "parallel", "arbitrary")),
        name="attn_prompt",
    )(*([z3] * 15), bias)
    return a.reshape(b * s, HEADS * HD_A)


def _attn_sample_kernel(q_ref, k_ref, v_ref, c0_ref, c1_ref, c2_ref, brow_ref, bself_ref, a_ref):
    caches = (c0_ref, c1_ref, c2_ref)
    gw = HEADS * HD_A
    for h in range(HEADS):
        outs, lses = [], []
        for g in range(N_GROUPS):
            col = g * gw + h * HD_A
            q = q_ref[:, col:col + HD_A]
            kn = k_ref[:, col:col + HD_A]
            vn = v_ref[:, col:col + HD_A]
            kc = caches[g][:, 0, h, :]
            vc = caches[g][:, 1, h, :]
            sc = jnp.sum(kc * q, axis=-1, keepdims=True) + brow_ref[g, :, h:h + 1]
            s0 = jnp.sum(kn * q, axis=-1, keepdims=True) + bself_ref[g, :, h:h + 1][0:1]
            m = jnp.maximum(jnp.max(sc, axis=0, keepdims=True), s0)
            e = jnp.exp(sc - m)
            e0 = jnp.exp(s0 - m)
            den = jnp.sum(e, axis=0, keepdims=True) + e0
            outs.append((jnp.sum(e * vc, axis=0, keepdims=True) + e0 * vn) / den)
            lses.append(m + jnp.log(den))
        mm = jnp.maximum(jnp.maximum(lses[0], lses[1]), lses[2])
        ws = [jnp.exp(l - mm) for l in lses]
        wsum = ws[0] + ws[1] + ws[2]
        a_ref[:, h * HD_A:(h + 1) * HD_A] = (ws[0] * outs[0] + ws[1] * outs[1] + ws[2] * outs[2]) / wsum


def _attn_sample(qa, ka, va, caches, layer, brow, bself):
    db = qa.shape[0]
    gw = HEADS * HD_A
    cviews, cspecs = [], []
    for c, (window, d) in zip(caches, ATTN_GROUPS):
        assert c.shape[2] == window, "sample attention expects a full window of cached rows"
        cviews.append(c.reshape(c.shape[0], db, window // d, d, 2, HEADS, HD_A))
        cspecs.append(pl.BlockSpec((None, None, ATTN_BLK, None, 2, HEADS, HD_A), lambda b: (layer, b, 0, 0, 0, 0, 0)))
    tok = pl.BlockSpec((None, 1, QA_W), lambda b: (b, 0, 0))
    return pl.pallas_call(
        _attn_sample_kernel,
        grid=(db,),
        in_specs=[tok, tok, tok] + cspecs + [
            pl.BlockSpec((N_GROUPS, ATTN_BLK, HEADS), lambda b: (0, 0, 0)),
            pl.BlockSpec((N_GROUPS, 8, HEADS), lambda b: (0, 0, 0)),
        ],
        out_specs=pl.BlockSpec((None, 1, gw), lambda b: (b, 0, 0)),
        out_shape=jax.ShapeDtypeStruct((db, 1, gw), F32),
        compiler_params=_cparams(("parallel",)),
        name="attn_sample",
    )(qa, ka, va, *cviews, brow, bself)


def _ret_epilogue(o, gate):
    on = o * lax.rsqrt(jnp.mean(o * o, axis=-1, keepdims=True) + EPS)
    return on * (gate / (1.0 + jnp.exp(-gate)))


def _ret_prompt_kernel(q_ref, k_ref, va_ref, vb_ref, ga_ref, gb_ref, dec_ref, xi_ref, zeta_ref, o_ref, rout_ref, r_scr):
    c = pl.program_id(0)
    heads_per_tile = COL_TILE // DV_R
    nb = q_ref.shape[0]

    @pl.when(c == 0)
    def _():
        r_scr[...] = jnp.zeros_like(r_scr)

    pending = []
    for bb in range(nb):
        for h in range(HR):
            qs = slice(h * DK_R, (h + 1) * DK_R)
            ts = slice((h % heads_per_tile) * DV_R, (h % heads_per_tile + 1) * DV_R)
            v_ref, g_ref = ((va_ref, ga_ref), (vb_ref, gb_ref))[h // heads_per_tile]
            k = k_ref[bb, :, qs]
            qb = q_ref[bb, :, qs].astype(BF16)
            vb = v_ref[bb, :, ts].astype(BF16)
            r = r_scr[bb, h]
            xi = xi_ref[h]
            s = _dot_nt(qb, k.astype(BF16)) * dec_ref[h]
            cross = _dot(qb, r.astype(BF16)) * xi
            kz = (k * zeta_ref[h]).T
            r_scr[bb, h] = xi[RET_CHUNK - 1:RET_CHUNK, :] * r + _dot(kz.astype(BF16), vb)
            pending.append((bb, h, ts, g_ref, s.astype(BF16), vb, cross))
    for bb, h, ts, g_ref, sb, vb, cross in pending:
        o = _dot(sb, vb) + cross
        o_ref[bb, :, h * DV_R:(h + 1) * DV_R] = _ret_epilogue(o, g_ref[bb, :, ts])

    @pl.when(c == pl.num_programs(0) - 1)
    def _():
        rout_ref[...] = r_scr[...]


def _ret_prompt(z3, dec, xi, zeta, b, s):
    nc = s // RET_CHUNK
    qk_w = HR * DK_R
    tab = lambda w: pl.BlockSpec((HR, RET_CHUNK, w), lambda c: (0, 0, 0))
    col = lambda w, tile: pl.BlockSpec((b, RET_CHUNK, w), lambda c: (0, c, tile))
    o, r = pl.pallas_call(
        _ret_prompt_kernel,
        grid=(nc,),
        in_specs=[
            col(qk_w, CT_QR), col(qk_w, CT_KR),
            col(COL_TILE, CT_VR), col(COL_TILE, CT_VR + 1), col(COL_TILE, CT_GR), col(COL_TILE, CT_GR + 1),
            tab(RET_CHUNK), tab(DV_R), tab(DK_R),
        ],
        out_specs=[
            pl.BlockSpec((b, RET_CHUNK, RV_W), lambda c: (0, c, 0)),
            pl.BlockSpec((b, HR, DK_R, DV_R), lambda c: (0, 0, 0, 0)),
        ],
        out_shape=[jax.ShapeDtypeStruct((b, s, RV_W), F32), jax.ShapeDtypeStruct((b, HR, DK_R, DV_R), F32)],
        scratch_shapes=[pltpu.VMEM((b, HR, DK_R, DV_R), F32)],
        compiler_params=_cparams(("arbitrary",)),
        name="ret_prompt",
    )(z3, z3, z3, z3, z3, z3, dec, xi, zeta)
    return o.reshape(b * s, RV_W), r


def _ret_sample_kernel(q_ref, k_ref, v_ref, g_ref, r0_ref, gam_ref, o_ref, rout_ref):
    eye = (lax.broadcasted_iota(jnp.int32, (DK_R, DK_R), 0) == lax.broadcasted_iota(jnp.int32, (DK_R, DK_R), 1))
    for h in range(HR):
        qs = slice(h * DK_R, (h + 1) * DK_R)
        vs = slice(h * DV_R, (h + 1) * DV_R)
        q = q_ref[:, qs]
        k = k_ref[:, qs]
        v = v_ref[:, vs]
        r = r0_ref[h]
        gam = gam_ref[h]
        qcol = jnp.sum(jnp.where(eye, jnp.broadcast_to(q, (DK_R, DK_R)), 0.0), axis=-1, keepdims=True)
        kcol = jnp.sum(jnp.where(eye, jnp.broadcast_to(k, (DK_R, DK_R)), 0.0), axis=-1, keepdims=True)
        qk = jnp.sum(q * k, axis=-1, keepdims=True)
        o = qk * v + jnp.sum(qcol * r, axis=0, keepdims=True) * gam
        rout_ref[h] = gam * r + kcol * v
        o_ref[:, vs] = _ret_epilogue(o, g_ref[:, vs])


def _ret_sample(qr, kr, vr, gr, state, layer, gam):
    db = qr.shape[0]
    qk_w = HR * DK_R
    return pl.pallas_call(
        _ret_sample_kernel,
        grid=(db,),
        in_specs=[
            pl.BlockSpec((None, 1, qk_w), lambda b: (b, 0, 0)),
            pl.BlockSpec((None, 1, qk_w), lambda b: (b, 0, 0)),
            pl.BlockSpec((None, 1, RV_W), lambda b: (b, 0, 0)),
            pl.BlockSpec((None, 1, RV_W), lambda b: (b, 0, 0)),
            pl.BlockSpec((None, None, HR, DK_R, DV_R), lambda b: (layer, b, 0, 0, 0)),
            pl.BlockSpec((HR, 1, DV_R), lambda b: (0, 0, 0)),
        ],
        out_specs=[
            pl.BlockSpec((None, 1, RV_W), lambda b: (b, 0, 0)),
            pl.BlockSpec((None, HR, DK_R, DV_R), lambda b: (b, 0, 0, 0)),
        ],
        out_shape=[jax.ShapeDtypeStruct((db, 1, RV_W), F32), jax.ShapeDtypeStruct((db, HR, DK_R, DV_R), F32)],
        compiler_params=_cparams(("parallel",)),
        name="ret_sample",
    )(qr, kr, vr, gr, state, gam)


def _merge_kernel(a_ref, ro_ref, ga0, ga1, gb0, gb1, x_ref, woa_ref, wor_ref, wout_ref, wq_ref, bg_ref, n2_ref,
                  x1_ref, h2_ref, pq_ref):
    a = a_ref[...]
    ba = _dot(a.astype(BF16), woa_ref[...])
    br = _dot(ro_ref[...].astype(BF16), wor_ref[...])
    ga = jnp.concatenate([ga0[...], ga1[...]], axis=-1) + bg_ref[:, :D_MODEL]
    gb = jnp.concatenate([gb0[...], gb1[...]], axis=-1) + bg_ref[:, D_MODEL:]
    mixed = ba / (1.0 + jnp.exp(-ga)) + br / (1.0 + jnp.exp(-gb))
    x1 = x_ref[...] + _dot(mixed.astype(BF16), wout_ref[...])
    x1_ref[...] = x1
    h2 = (x1 * lax.rsqrt(jnp.mean(x1 * x1, axis=-1, keepdims=True) + EPS) * n2_ref[...]).astype(BF16)
    h2_ref[...] = h2
    pq_ref[...] = _dot(h2, wq_ref[...])


def _merge(a, ro, z, x, woa, wor, wout, wq, bg, n2, tm):
    t = x.shape[0]
    gw = HEADS * HD_A
    row = lambda w, c=0: pl.BlockSpec((tm, w), lambda i, c=c: (i, c))
    full = lambda a: pl.BlockSpec(a.shape, lambda i: (0, 0), pipeline_mode=pl.Buffered(1))
    in_specs = [
        row(gw), row(RV_W),
        row(COL_TILE, CT_GATE), row(COL_TILE, CT_GATE + 1), row(COL_TILE, CT_GATE + 2), row(COL_TILE, CT_GATE + 3),
        row(D_MODEL),
        full(woa), full(wor), full(wout), full(wq), full(bg), full(n2),
    ]
    return pl.pallas_call(
        _merge_kernel,
        grid=(t // tm,),
        in_specs=in_specs,
        out_specs=[row(D_MODEL), row(D_MODEL), row(wq.shape[1])],
        out_shape=[jax.ShapeDtypeStruct((t, D_MODEL), F32), jax.ShapeDtypeStruct((t, D_MODEL), BF16),
                   jax.ShapeDtypeStruct((t, wq.shape[1]), F32)],
        compiler_params=_cparams(("parallel",)),
        name="merge",
    )(a, ro, z, z, z, z, x, woa, wor, wout, wq, bg, n2)


def _topk_rank(s, break_ties):
    key = lax.broadcasted_iota(jnp.int32, s.shape, 0).astype(F32)
    rank = jnp.full(s.shape, float(PEER_TOPK), F32)
    vals = []
    for k in range(PEER_TOPK):
        m = jnp.max(s, axis=0, keepdims=True)
        hit = s == m
        if break_ties:
            hit = key == jnp.min(jnp.where(hit, key, float(N_KEYS)), axis=0, keepdims=True)
        rank = jnp.where(hit, float(k), rank)
        s = jnp.where(hit, -jnp.inf, s)
        vals.append(m)
    return rank, vals


def _column_sort_network(n):
    pairs = []
    p = 1
    while p < n:
        k = p
        while k >= 1:
            for j in range(k % p, n - k, 2 * k):
                for i in range(min(k, n - j - k)):
                    if (i + j) // (2 * p) == (i + j + k) // (2 * p):
                        pairs.append((i + j, i + j + k))
            k //= 2
        p *= 2
    return pairs


def _top16_values(s):
    cols = [s[v * SUBLANES:(v + 1) * SUBLANES] for v in range(N_KEYS // SUBLANES)]
    for i, j in _column_sort_network(len(cols)):
        cols[i], cols[j] = jnp.maximum(cols[i], cols[j]), jnp.minimum(cols[i], cols[j])
    vals = []
    for _ in range(PEER_TOPK):
        m = jnp.max(cols[0], axis=0, keepdims=True)
        hit = cols[0] == m
        for v in range(len(cols) - 1):
            cols[v] = jnp.where(hit, cols[v + 1], cols[v])
        cols[-1] = jnp.where(hit, -jnp.inf, cols[-1])
        vals.append(m)
    moved = None
    for c in cols:
        gone = jnp.sum(jnp.where(c == -jnp.inf, 1.0, 0.0), axis=0, keepdims=True)
        moved = gone if moved is None else moved + gone
    tie = jnp.where(moved != float(PEER_TOPK), 1.0, 0.0)
    nxt = jnp.max(cols[0], axis=0, keepdims=True)
    for hi_v, lo_v in zip(vals, vals[1:] + [nxt]):
        tie = tie + jnp.where(hi_v == lo_v, 1.0, 0.0)
    return vals, tie


def _exactly_16(picked):
    return jnp.where(jnp.sum(picked, axis=0, keepdims=True) != float(PEER_TOPK), 1.0, 0.0)


def _peer_select_head(pq_ref, sk_ref, row_ref, tile_ref, hd, break_ties):
    ntok = pq_ref.shape[0]
    hi = lax.Precision.HIGHEST
    sub = lax.broadcasted_iota(jnp.int32, (SUBLANES, ntok), 0).astype(F32)
    flat = jnp.concatenate([sub, 8.0 + sub, 16.0 + sub, 32.0 + sub, 48.0 + sub,
                            (8.0 + sub) * 16.0, sub * 16.0, sub * 16.0 + 1.0, sub * 16.0 + 2.0], axis=0)
    ninf = jnp.full((SUBLANES, ntok), -jnp.inf, F32)
    sc = []
    for p in range(2):
        col = (hd * 2 + p) * N_KEYS
        sc.append(_dot_nt(sk_ref[p], pq_ref[:, col:col + N_KEYS], hi))
    if break_ties:
        rank1, sv1 = _topk_rank(sc[0], True)
        rank2, sv2 = _topk_rank(sc[1], True)
    else:
        sv1, tie1 = _top16_values(sc[0])
        sv2, tie2 = _top16_values(sc[1])
    a_lo, a_hi = jnp.concatenate(sv1[:8], axis=0), jnp.concatenate(sv1[8:], axis=0)
    b_lo, b_hi = jnp.concatenate(sv2[:8], axis=0), jnp.concatenate(sv2[8:], axis=0)
    cand = jnp.concatenate([
        sv1[0] + b_lo, sv1[0] + b_hi, sv1[1] + b_lo,
        jnp.where(sub < 5.0, sv1[2] + b_lo, ninf),
        jnp.where(sub < 4.0, sv1[3] + b_lo, ninf),
        a_hi + sv2[0],
        jnp.where(sub >= 4.0, a_lo + sv2[0], ninf),
        jnp.where(sub >= 4.0, a_lo + sv2[1], ninf),
        jnp.where(sub == 4.0, a_lo + sv2[2], ninf),
    ], axis=0)
    work = cand
    chosen = jnp.zeros(cand.shape, F32)
    for _ in range(PEER_TOPK):
        m = jnp.max(work, axis=0, keepdims=True)
        hit = work == m
        if break_ties:
            hit = flat == jnp.min(jnp.where(hit, flat, float(PEER_TOPK * PEER_TOPK)), axis=0, keepdims=True)
        chosen = jnp.where(hit, 1.0, chosen)
        work = jnp.where(hit, -jnp.inf, work)
    top = sv1[0] + sv2[0]
    zsum = jnp.sum(chosen * jnp.exp(cand - top), axis=0, keepdims=True)
    cnt_rows = [jnp.sum(chosen[0:16], axis=0, keepdims=True)]
    cnt_rows += [jnp.sum(chosen[8 * g:8 * g + 8], axis=0, keepdims=True) for g in (2, 3, 4)]
    cnt_mid = chosen[48:56] + chosen[56:64] + chosen[64:72]
    cnt_rows += [cnt_mid[a:a + 1] for a in range(4, 8)]
    cnt_rows += [chosen[40 + a:41 + a] for a in range(8)]
    row_ref[hd, 1] = 0.5 * jnp.exp(sc[0] - sv1[0]) / zsum
    tile_ref[hd, 1] = jnp.exp(sc[1] - sv2[0])
    if break_ties:
        cnt1 = jnp.zeros((N_KEYS, ntok), F32)
        for a in range(PEER_TOPK):
            cnt1 = jnp.where(rank1 == float(a), cnt_rows[a], cnt1)
        row_ref[hd, 0] = 1.0 - cnt1
        tile_ref[hd, 0] = -rank2
        return None
    thr1 = jnp.full((N_KEYS, ntok), jnp.inf, F32)
    for a in range(PEER_TOPK):
        thr_a = jnp.full((1, ntok), jnp.inf, F32)
        for b in range(PEER_TOPK // (a + 1)):
            thr_a = jnp.where(cnt_rows[a] == float(b + 1), sv2[b], thr_a)
        thr1 = jnp.where(sc[0] == sv1[a], thr_a, thr1)
    row_ref[hd, 0] = thr1
    tile_ref[hd, 0] = sc[1]
    return tie1 + tie2 + _exactly_16(chosen)


def _peer_select_kernel(pq_ref, sk_ref, row_ref, tile_ref):
    ties = None
    for hd in range(PEER_HEADS):
        t = _peer_select_head(pq_ref, sk_ref, row_ref, tile_ref, hd, break_ties=False)
        ties = t if ties is None else ties + t

    @pl.when(jnp.max(ties) > 0.0)
    def _():
        for hd in range(PEER_HEADS):
            _peer_select_head(pq_ref, sk_ref, row_ref, tile_ref, hd, break_ties=True)


def _peer_select(pq, sk):
    t = pq.shape[0]
    tl = PEER_TOK_LANES
    shape = (t // tl, PEER_HEADS, 2, N_KEYS, tl)
    spec = pl.BlockSpec((None, PEER_HEADS, 2, N_KEYS, tl), lambda i: (i, 0, 0, 0, 0))
    return pl.pallas_call(
        _peer_select_kernel,
        grid=(t // tl,),
        in_specs=[
            pl.BlockSpec((tl, pq.shape[1]), lambda i: (i, 0)),
            pl.BlockSpec(sk.shape, lambda i: (0, 0, 0)),
        ],
        out_specs=[spec, spec],
        out_shape=[jax.ShapeDtypeStruct(shape, F32), jax.ShapeDtypeStruct(shape, F32)],
        compiler_params=_cparams(("parallel",)),
        name="peer_select",
    )(pq, sk)


def _gelu_x2(x):
    return x * (1.0 + lax.erf(x * np.float32(2.0 ** -0.5)))


def _peer_dense_kernel(h_ref, row_ref, tile_ref, u_ref, vt_ref, x_ref, out_ref, act_ref, g_ref, acc_ref, *, nsub):
    e = pl.program_id(1)
    keys_per_blk = PEER_EXPERT_BLK // N_KEYS
    groups = N_KEYS // SUBLANES
    vreg = (SUBLANES, PEER_TOK_LANES)

    @pl.when(e == 0)
    def _():
        acc_ref[...] = jnp.zeros_like(acc_ref)

    act_ref[...] = _gelu_x2(_dot_nt(u_ref[...], h_ref[...]))

    def gate(kk, carry):
        i = e * keys_per_blk + kk
        row0 = pl.multiple_of(kk * N_KEYS, N_KEYS)
        for t in range(nsub):
            lanes = slice(t * PEER_TOK_LANES, (t + 1) * PEER_TOK_LANES)
            w = [None] * groups
            for hd in range(PEER_HEADS):
                thr = jnp.broadcast_to(row_ref[t, hd, 0, pl.ds(i, 1), :], vreg)
                e1 = jnp.broadcast_to(row_ref[t, hd, 1, pl.ds(i, 1), :], vreg)
                for g in range(groups):
                    keys = slice(g * SUBLANES, (g + 1) * SUBLANES)
                    term = jnp.where(tile_ref[t, hd, 0, keys, :] >= thr, e1 * tile_ref[t, hd, 1, keys, :], 0.0)
                    w[g] = term if w[g] is None else w[g] + term
            rows = pl.ds(row0, N_KEYS)
            g_ref[rows, lanes] = (jnp.concatenate(w, axis=0) * act_ref[rows, lanes]).astype(BF16)
        return carry

    lax.fori_loop(0, keys_per_blk, gate, 0)
    acc_ref[...] += _dot(vt_ref[...], g_ref[...])

    @pl.when(e == pl.num_programs(1) - 1)
    def _():
        out_ref[...] = x_ref[...] + acc_ref[...].T


def _peer_dense(h2, sel_rows, sel_tiles, u_bf, vt_bf, x1, tb):
    t = h2.shape[0]
    nsub = tb // PEER_TOK_LANES
    eb = PEER_EXPERT_BLK
    sel_spec = pl.BlockSpec((nsub, PEER_HEADS, 2, N_KEYS, PEER_TOK_LANES), lambda i, e: (i, 0, 0, 0, 0))
    return pl.pallas_call(
        functools.partial(_peer_dense_kernel, nsub=nsub),
        grid=(t // tb, N_EXPERTS // eb),
        in_specs=[
            pl.BlockSpec((tb, D_MODEL), lambda i, e: (i, 0)),
            sel_spec,
            sel_spec,
            pl.BlockSpec((eb, D_MODEL), lambda i, e: (e, 0)),
            pl.BlockSpec((D_MODEL, eb), lambda i, e: (0, e)),
            pl.BlockSpec((tb, D_MODEL), lambda i, e: (i, 0)),
        ],
        out_specs=pl.BlockSpec((tb, D_MODEL), lambda i, e: (i, 0)),
        out_shape=jax.ShapeDtypeStruct((t, D_MODEL), F32),
        scratch_shapes=[
            pltpu.VMEM((eb, tb), F32),
            pltpu.VMEM((eb, tb), BF16),
            pltpu.VMEM((D_MODEL, tb), F32),
        ],
        compiler_params=_cparams(("parallel", "arbitrary")),
        name="peer_dense",
    )(h2, sel_rows, sel_tiles, u_bf, vt_bf, x1)


def _peer(h2, pq, x1, sk, u_bf, vt_bf):
    t = h2.shape[0]
    tb = min(512, -(-t // PEER_TOK_LANES) * PEER_TOK_LANES)
    t_pad = -(-t // tb) * tb
    if t_pad != t:
        h2 = jnp.pad(h2, ((0, t_pad - t), (0, 0)))
        x1 = jnp.pad(x1, ((0, t_pad - t), (0, 0)))
        pq = jnp.pad(pq, ((0, t_pad - t), (0, 0)))
    sel_rows, sel_tiles = _peer_select(pq, sk)
    return _peer_dense(h2, sel_rows, sel_tiles, u_bf, vt_bf, x1, tb)[:t]


def _t5_bucket(dist):
    dist = np.asarray(dist, np.int64)
    max_exact = N_BUCKETS // 2
    ratio = np.log(np.maximum(dist, 1) / max_exact) / np.log(REL_MAX_DIST / max_exact)
    large = np.minimum(max_exact + (ratio * (N_BUCKETS - max_exact)).astype(np.int64), N_BUCKETS - 1)
    return np.where(dist < max_exact, dist, large).astype(np.int32)


def _rotary_tables(pos):
    half = DK_R // 2
    ang = jnp.asarray(1.0 / (10000.0 ** np.linspace(0.0, 1.0, half)), F32)
    theta = pos.astype(F32)[:, None] * ang[None, :]
    c, s = jnp.cos(theta), jnp.sin(theta)
    return jnp.concatenate([c, c], axis=-1), jnp.concatenate([-s, s], axis=-1)


def _retention_tables(length, log_g):
    n = np.arange(length)
    diff = n[:, None] - n[None, :]
    decay = jnp.where(jnp.asarray(diff >= 0)[None],
                      jnp.exp(jnp.asarray(np.maximum(diff, 0), F32)[None] * log_g[:, None, None]), 0.0)
    xi = jnp.exp(jnp.asarray(n + 1, F32)[None, :] * log_g[:, None])
    zeta = jnp.exp(jnp.asarray(n[::-1].copy(), F32)[None, :] * log_g[:, None])
    xi = jnp.broadcast_to(xi[:, :, None], (HR, length, DV_R))
    zeta = jnp.broadcast_to(zeta[:, :, None], (HR, length, DK_R))
    return decay, xi, zeta


def _prompt_bias(bias_g, g, d):
    steps = ATTN_GROUPS[g][0] // d
    assert steps == ATTN_BLK
    per_dist = bias_g[:, g][_t5_bucket(np.arange(steps + 1) * d)]
    n = 3 * ATTN_BLK
    diag = per_dist[np.clip(2 * ATTN_BLK - 1 - np.arange(n), 0, steps)].T
    rows = jnp.tile(diag, (1, ATTN_BLK + 1))[:, :ATTN_BLK * (n + 1)].reshape(HEADS, ATTN_BLK, n + 1)
    return rows[:, ::-1, :2 * ATTN_BLK]


def _sample_bias(bias_g):
    rows, selfs = [], []
    for g, (window, d) in enumerate(ATTN_GROUPS):
        steps = window // d
        assert steps == ATTN_BLK
        j = steps - np.arange(steps)
        rows.append(bias_g[:, g][_t5_bucket(j * d)])
        selfs.append(jnp.broadcast_to(bias_g[:, g][_t5_bucket(np.zeros(1))], (8, HEADS)))
    return jnp.stack(rows), jnp.stack(selfs)


def kernel(x_prompt, x_sample, cache_kv_d1, cache_kv_d4, cache_kv_d16, state_ret, rel_bias, norm1_w, w_in, b_gate, q_norm_w, k_norm_w, w_o_attn, w_o_ret, w_out, norm2_w, peer_w_q, peer_sub_keys, peer_u, peer_v):
    b, s, _ = x_prompt.shape
    db, ds, _ = x_sample.shape
    assert ds == 1, "the sample group is one new token per sequence"
    depth = w_in.shape[0]
    gw = HEADS * HD_A
    caches = (cache_kv_d1, cache_kv_d4, cache_kv_d16)

    bias_g = rel_bias.reshape(N_BUCKETS, N_GROUPS, HEADS).astype(F32)
    log_g = jnp.asarray(np.log(1.0 - 2.0 ** (-5.0 - np.arange(HR))), F32)
    cos_p, sin_p = _rotary_tables(jnp.arange(s, dtype=jnp.int32))
    cos_s, sin_s = _rotary_tables(jnp.full((db,), PAST_LEN, jnp.int32))
    dec_p, xi_p, zeta_p = _retention_tables(RET_CHUNK, log_g)
    gam_s = _retention_tables(1, log_g)[1]
    bias_p = jnp.stack([_prompt_bias(bias_g, g, d) for g, (_, d) in enumerate(ATTN_GROUPS)])
    brow_s, bself_s = _sample_bias(bias_g)

    assert s % ATTN_ROWS == 0 and s % RET_CHUNK == 0, "prompt length must be a multiple of the attention step"
    tm_p = next(t for t in (2048, 1024, 512, ATTN_BLK) if s % t == 0)
    xp = x_prompt.reshape(b * s, D_MODEL)
    xs = x_sample.reshape(db, D_MODEL)
    kv_p = [[] for _ in range(N_GROUPS)]
    kv_s = [[] for _ in range(N_GROUPS)]
    ret_p, ret_s = [], []

    for l in range(depth):
        w_in_bf = w_in[l].astype(BF16)
        ones = jnp.ones((COL_TILE,), F32)
        colw = jnp.stack(
            [jnp.tile(q_norm_w[l, g], HEADS) * (HD_A ** -0.5) for g in range(N_GROUPS)]
            + [jnp.tile(k_norm_w[l, g], HEADS) for g in range(N_GROUPS)]
            + [ones] * (CT_KR - CT_V) + [ones * (DK_R ** -0.5)] + [ones] * (N_COL_TILES - CT_KR - 1)
        ).reshape(N_COL_TILES, 1, COL_TILE)
        n1 = norm1_w[l].reshape(1, D_MODEL)
        n2 = norm2_w[l].reshape(1, D_MODEL)
        bg = b_gate[l].reshape(1, 2 * D_MODEL)
        woa, wor, wout = w_o_attn[l].astype(BF16), w_o_ret[l].astype(BF16), w_out[l].astype(BF16)
        wq = peer_w_q[l].astype(BF16)
        sk = peer_sub_keys[l]
        u_bf = peer_u[l].astype(BF16)
        vt_bf = peer_v[l].T.astype(BF16)

        zp = _inproj(xp, n1, w_in_bf, colw, cos_p, sin_p, tm_p, s // tm_p)
        zp3 = zp.reshape(b, s, IN_COLS)
        a_p = _attn_prompt(zp3, bias_p, b, s)
        for g, (window, d) in enumerate(ATTN_GROUPS):
            keep = min(window, s)
            k_rows = zp3[:, s - keep:, QA_W + g * gw:QA_W + (g + 1) * gw].reshape(b, keep, HEADS, HD_A)
            v_rows = zp3[:, s - keep:, 2 * QA_W + g * gw:2 * QA_W + (g + 1) * gw].reshape(b, keep, HEADS, HD_A)
            kv_p[g].append(jnp.stack([k_rows, v_rows], axis=2))
        ro, r_new = _ret_prompt(zp3, dec_p, xi_p, zeta_p, b, s)
        ret_p.append(r_new)
        x1, h2, pq = _merge(a_p, ro, zp, xp, woa, wor, wout, wq, bg, n2, next(t for t in (512, 256, ATTN_BLK) if (b * s) % t == 0))
        xp = _peer(h2, pq, x1, sk, u_bf, vt_bf)

        zs = _inproj(xs, n1, w_in_bf, colw, cos_s, sin_s, db, 1)
        zs3 = zs.reshape(db, 1, IN_COLS)
        qa_s, ka_s, va_s = zs3[:, :, :QA_W], zs3[:, :, QA_W:2 * QA_W], zs3[:, :, 2 * QA_W:3 * QA_W]
        c0 = 3 * QA_W
        qr_s, kr_s = zs3[:, :, c0:c0 + RQ_W], zs3[:, :, c0 + RQ_W:c0 + 2 * RQ_W]
        c1 = c0 + 2 * RQ_W
        vr_s, gr_s = zs3[:, :, c1:c1 + RV_W], zs3[:, :, c1 + RV_W:c1 + 2 * RV_W]
        a_s = _attn_sample(qa_s, ka_s, va_s, caches, l, brow_s, bself_s)
        for g in range(N_GROUPS):
            k_rows = ka_s[:, :, g * gw:(g + 1) * gw].reshape(db, 1, HEADS, HD_A)
            v_rows = va_s[:, :, g * gw:(g + 1) * gw].reshape(db, 1, HEADS, HD_A)
            kv_s[g].append(jnp.stack([k_rows, v_rows], axis=2))
        ro_s, r_s = _ret_sample(qr_s, kr_s, vr_s, gr_s, state_ret, l, gam_s)
        ret_s.append(r_s)
        x1s, h2s, pqs = _merge(a_s.reshape(db, gw), ro_s.reshape(db, RV_W), zs, xs, woa, wor, wout, wq, bg, n2, db)
        xs = _peer(h2s, pqs, x1s, sk, u_bf, vt_bf)

    return (xp.reshape(b, s, D_MODEL), xs.reshape(db, 1, D_MODEL),
            jnp.stack(kv_p[0]), jnp.stack(kv_p[1]), jnp.stack(kv_p[2]), jnp.stack(ret_p),
            jnp.stack(kv_s[0]), jnp.stack(kv_s[1]), jnp.stack(kv_s[2]), jnp.stack(ret_s))
```
